```python
import jax
import jax.numpy as jnp
from jax import lax
import numpy as np

D_MODEL = 2048
BATCH = 2
SEQ = 4096
DEPTH = 1

GRID_W = 64
CTX_LEN = 256
HEAD_DIM = 128
ATTN_WIDTH = D_MODEL // 2
N_HEADS = ATTN_WIDTH // HEAD_DIM
N_KV_HEADS = 2
Q_PER_KV = N_HEADS // N_KV_HEADS
KV_WIDTH = N_KV_HEADS * HEAD_DIM
POOL_WIDTH = D_MODEL - ATTN_WIDTH
MIX_WIDTH = POOL_WIDTH + ATTN_WIDTH
POOL_WINDOWS = (2, 4, 8, 16)
N_POOL_GROUPS = len(POOL_WINDOWS)
POOL_GROUP = POOL_WIDTH // N_POOL_GROUPS
Q_OFF = POOL_WIDTH
K_OFF = Q_OFF + ATTN_WIDTH
V_OFF = K_OFF + KV_WIDTH
IN_WIDTH = V_OFF + KV_WIDTH
ROPE_AXIS_DIM = HEAD_DIM // 2
ROPE_THETA = 10000.0
ATTN_SCALE = HEAD_DIM ** -0.5
Q_BLOCK = 128
N_EXPERTS = 16
CAPACITY_FACTOR = 2
EXPERT_FF = 5632
LN_EPS = 1e-6
QK_EPS = 1e-6
ALPHA = (2.0 * DEPTH) ** 0.25
BETA = (8.0 * DEPTH) ** -0.25

kernel_name = 'hybrid_pool_gqa_ec_moe_dit_block'


def layer_norm(x, g, b):
    xf = x.astype(jnp.float32)
    mu = jnp.mean(xf, -1, keepdims=True)
    var = jnp.mean(jnp.square(xf - mu), -1, keepdims=True)
    return ((xf - mu) * lax.rsqrt(var + LN_EPS) * g.astype(jnp.float32) + b.astype(jnp.float32)).astype(x.dtype)


def head_rms_norm(x, g):
    xf = x.astype(jnp.float32)
    return (xf * lax.rsqrt(jnp.mean(jnp.square(xf), -1, keepdims=True) + QK_EPS) * g.astype(jnp.float32)).astype(x.dtype)


def axial_rope_tables(n_tokens):
    rows = n_tokens // GRID_W
    row = jnp.broadcast_to(jnp.arange(rows, dtype=jnp.float32)[:, None], (rows, GRID_W)).reshape(-1)
    col = jnp.broadcast_to(jnp.arange(GRID_W, dtype=jnp.float32)[None, :], (rows, GRID_W)).reshape(-1)
    inv_freq = ROPE_THETA ** (-jnp.arange(0, ROPE_AXIS_DIM, 2, dtype=jnp.float32) / ROPE_AXIS_DIM)
    ang_r = row[:, None] * inv_freq[None, :]
    ang_c = col[:, None] * inv_freq[None, :]
    ang = jnp.concatenate([ang_r, ang_r, ang_c, ang_c], axis=-1)
    return jnp.cos(ang), jnp.sin(ang)


def apply_axial_rope(x, cos, sin):
    xr = x.reshape(x.shape[:-1] + (2, 2, ROPE_AXIS_DIM // 2))
    rot = jnp.concatenate([-xr[..., 1:2, :], xr[..., 0:1, :]], axis=-2).reshape(x.shape)
    return (x * cos[None, :, None, :] + rot * sin[None, :, None, :]).astype(x.dtype)


def multiscale_pool_mixer(u, w_pool, scale):
    bsz, n, _ = u.shape
    uf = u.astype(jnp.float32)
    cs = jnp.concatenate([jnp.zeros((bsz, 1, POOL_WIDTH), jnp.float32), jnp.cumsum(uf, axis=1)], axis=1)
    t = np.arange(n)
    means = []
    for gi, w in enumerate(POOL_WINDOWS):
        lo = np.clip(t - w // 2, 0, n)
        hi = np.clip(t + w // 2, 0, n)
        cnt = (hi - lo).astype(np.float32)
        seg = cs[:, :, gi * POOL_GROUP:(gi + 1) * POOL_GROUP]
        means.append((seg[:, hi] - seg[:, lo]) / cnt[None, :, None])
    pooled = jnp.concatenate(means, axis=-1)
    diff = (pooled - uf).astype(u.dtype).reshape(bsz, n, N_POOL_GROUPS, POOL_GROUP)
    mixed = jnp.einsum('bngc,gcd->bngd', diff, w_pool).reshape(bsz, n, POOL_WIDTH)
    return mixed * scale


def latent_attention(q, k, v, k_ctx, v_ctx):
    bsz, n = q.shape[:2]
    nb = n // Q_BLOCK
    k_all = jnp.concatenate([k_ctx, k], axis=1)
    v_all = jnp.concatenate([v_ctx, v], axis=1)
    qb = q.reshape(bsz, nb, Q_BLOCK, N_KV_HEADS, Q_PER_KV, HEAD_DIM).transpose(1, 0, 3, 4, 2, 5)

    def block(qi):
        s = jnp.einsum('bkgqd,bskd->bkgqs', qi, k_all).astype(jnp.float32) * ATTN_SCALE
        p = jax.nn.softmax(s, axis=-1).astype(v_all.dtype)
        return jnp.einsum('bkgqs,bskd->bkgqd', p, v_all)

    o = lax.map(block, qb)
    return o.transpose(1, 0, 4, 2, 3, 5).reshape(bsz, n, ATTN_WIDTH)


def context_attention(q, k, v):
    bsz, n = q.shape[:2]
    qg = q.reshape(bsz, n, N_KV_HEADS, Q_PER_KV, HEAD_DIM)
    s = jnp.einsum('bqkgd,bskd->bkgqs', qg, k).astype(jnp.float32) * ATTN_SCALE
    p = jax.nn.softmax(s, axis=-1).astype(v.dtype)
    return jnp.einsum('bkgqs,bskd->bqkgd', p, v).reshape(bsz, n, ATTN_WIDTH)


def expert_choice_moe(h, router_w, w_gate, w_up, w_down):
    bsz, n, _ = h.shape
    cap = CAPACITY_FACTOR * n // N_EXPERTS
    aff = jax.nn.softmax(jnp.einsum('bnd,de->bne', h, router_w).astype(jnp.float32), axis=-1)
    gates, idx = lax.top_k(aff.transpose(0, 2, 1), cap)
    bi = jnp.arange(bsz)[:, None, None]
    xe = h[bi, idx]
    a = jnp.einsum('becd,edf->becf', xe, w_gate)
    b = jnp.einsum('becd,edf->becf', xe, w_up)
    y = jnp.einsum('becf,efd->becd', jax.nn.silu(a) * b, w_down)
    y = y * gates[..., None].astype(y.dtype)
    return jnp.zeros_like(h).at[bi, idx].add(y)


def setup_inputs(seed: int = 0) -> dict:
    key = jax.random.key(seed)
    ks = jax.random.split(key, 20)

    def nrm(k, shape, s):
        return jax.random.normal(k, shape, jnp.float32) * s

    return {
        'x': nrm(ks[0], (BATCH, SEQ, D_MODEL), 1.0),
        'c': nrm(ks[1], (BATCH, D_MODEL), 1.0),
        'ctx': nrm(ks[2], (BATCH, CTX_LEN, D_MODEL), 1.0),
        'c_ctx': nrm(ks[3], (D_MODEL,), 1.0),
        'ada_w': nrm(ks[4], (DEPTH, D_MODEL, 6 * D_MODEL), 0.5 * D_MODEL ** -0.5),
        'ada_b': nrm(ks[5], (DEPTH, 6 * D_MODEL), 0.02),
        'w_in': nrm(ks[6], (DEPTH, D_MODEL, IN_WIDTH), D_MODEL ** -0.5),
        'q_norm': 1.0 + nrm(ks[7], (DEPTH, HEAD_DIM), 0.02),
        'k_norm': 1.0 + nrm(ks[8], (DEPTH, HEAD_DIM), 0.02),
        'pool_w': nrm(ks[9], (DEPTH, N_POOL_GROUPS, POOL_GROUP, POOL_GROUP), POOL_GROUP ** -0.5),
        'pool_scale': 1.0 + nrm(ks[10], (DEPTH, POOL_WIDTH), 0.1),
        'w_out': nrm(ks[11], (DEPTH, MIX_WIDTH, D_MODEL), BETA * MIX_WIDTH ** -0.5),
        'ln1_g': 1.0 + nrm(ks[12], (DEPTH, D_MODEL), 0.02),
        'ln1_b': nrm(ks[13], (DEPTH, D_MODEL), 0.02),
        'router_w': nrm(ks[14], (DEPTH, D_MODEL, N_EXPERTS), D_MODEL ** -0.5),
        'w_gate': nrm(ks[15], (DEPTH, N_EXPERTS, D_MODEL, EXPERT_FF), D_MODEL ** -0.5),
        'w_up': nrm(ks[16], (DEPTH, N_EXPERTS, D_MODEL, EXPERT_FF), D_MODEL ** -0.5),
        'w_down': nrm(ks[17], (DEPTH, N_EXPERTS, EXPERT_FF, D_MODEL), BETA * EXPERT_FF ** -0.5),
        'ln2_g': 1.0 + nrm(ks[18], (DEPTH, D_MODEL), 0.02),
        'ln2_b': nrm(ks[19], (DEPTH, D_MODEL), 0.02),
    }


def reference(x, c, ctx, c_ctx, ada_w, ada_b, w_in, q_norm, k_norm, pool_w, pool_scale, w_out,
              ln1_g, ln1_b, router_w, w_gate, w_up, w_down, ln2_g, ln2_b):
    bsz, n_tok = x.shape[:2]
    n_ctx = ctx.shape[1]
    cos, sin = axial_rope_tables(n_tok)
    for l in range(DEPTH):
        mod = jax.nn.silu(c) @ ada_w[l] + ada_b[l]
        sh1, sc1, g1, sh2, sc2, g2 = [m[:, None, :] for m in jnp.split(mod, 6, axis=-1)]
        mod_c = jax.nn.silu(c_ctx) @ ada_w[l] + ada_b[l]
        csh1, csc1, cg1, csh2, csc2, cg2 = jnp.split(mod_c, 6, axis=-1)

        h = x * (1.0 + sc1) + sh1
        hc = ctx * (1.0 + csc1) + csh1

        kvc = hc @ w_in[l][:, K_OFF:]
        kc = head_rms_norm(kvc[..., :KV_WIDTH].reshape(bsz, n_ctx, N_KV_HEADS, HEAD_DIM), k_norm[l])
        vc = kvc[..., KV_WIDTH:].reshape(bsz, n_ctx, N_KV_HEADS, HEAD_DIM)

        proj = h @ w_in[l]
        u = proj[..., :Q_OFF]
        q = head_rms_norm(proj[..., Q_OFF:K_OFF].reshape(bsz, n_tok, N_HEADS, HEAD_DIM), q_norm[l])
        k = head_rms_norm(proj[..., K_OFF:V_OFF].reshape(bsz, n_tok, N_KV_HEADS, HEAD_DIM), k_norm[l])
        v = proj[..., V_OFF:].reshape(bsz, n_tok, N_KV_HEADS, HEAD_DIM)
        q = apply_axial_rope(q, cos, sin)
        k = apply_axial_rope(k, cos, sin)

        mixed = jnp.concatenate([multiscale_pool_mixer(u, pool_w[l], pool_scale[l]),
                                 latent_attention(q, k, v, kc, vc)], axis=-1)
        y = mixed @ w_out[l]
        x_mid = layer_norm(ALPHA * x + g1 * y, ln1_g[l], ln1_b[l])

        h2 = x_mid * (1.0 + sc2) + sh2
        x_new = layer_norm(ALPHA * x_mid + g2 * expert_choice_moe(h2, router_w[l], w_gate[l], w_up[l], w_down[l]),
                           ln2_g[l], ln2_b[l])

        if l < DEPTH - 1:
            proj_c = hc @ w_in[l][:, :K_OFF]
            uc = proj_c[..., :Q_OFF]
            qc = head_rms_norm(proj_c[..., Q_OFF:].reshape(bsz, n_ctx, N_HEADS, HEAD_DIM), q_norm[l])
            mixed_c = jnp.concatenate([multiscale_pool_mixer(uc, pool_w[l], pool_scale[l]),
                                       context_attention(qc, kc, vc)], axis=-1)
            ctx_mid = layer_norm(ALPHA * ctx + cg1 * (mixed_c @ w_out[l]), ln1_g[l], ln1_b[l])
            hc2 = ctx_mid * (1.0 + csc2) + csh2
            ctx = layer_norm(ALPHA * ctx_mid + cg2 * expert_choice_moe(hc2, router_w[l], w_gate[l], w_up[l], w_down[l]),
                             ln2_g[l], ln2_b[l])
        x = x_new
    return x
```

```python
import functools

import jax
import jax.numpy as jnp
import numpy as np
from jax import lax
from jax.experimental import pallas as pl
from jax.experimental.pallas import tpu as pltpu

F32 = jnp.float32
BF16 = jnp.bfloat16

D_MODEL = 2048
GRID_W = 64
HEAD_DIM = 128
ATTN_WIDTH = D_MODEL // 2
N_HEADS = ATTN_WIDTH // HEAD_DIM
N_KV_HEADS = 2
Q_PER_KV = N_HEADS // N_KV_HEADS
KV_WIDTH = N_KV_HEADS * HEAD_DIM
POOL_WIDTH = D_MODEL - ATTN_WIDTH
POOL_WINDOWS = (2, 4, 8, 16)
POOL_GROUP = POOL_WIDTH // len(POOL_WINDOWS)
Q_OFF = POOL_WIDTH
K_OFF = Q_OFF + ATTN_WIDTH
V_OFF = K_OFF + KV_WIDTH
IN_WIDTH = V_OFF + KV_WIDTH
ROPE_AXIS_DIM = HEAD_DIM // 2
ROPE_THETA = 10000.0
ATTN_SCALE = HEAD_DIM ** -0.5
N_EXPERTS = 16
CAPACITY_FACTOR = 2
LN_EPS = 1e-6
QK_EPS = 1e-6
DEPTH = 1
ALPHA = (2.0 * DEPTH) ** 0.25

LANES = 128
SUBLANES = 8
V7X_VMEM_BYTES = 64 * 1024 * 1024
MOD_ROWS = SUBLANES
HALO = SUBLANES

ADA_TN = 1024
INPROJ_TM = 256
POOL_TM = 512
ATTN_TQ = 128
ATTN_TK = 512
OUTPROJ_TM = 256
MOE_TF = 256
FINAL_TM = 512
PREFIX_BLK = 256


def _vmem_limit(nbytes):
    return int(min(nbytes, V7X_VMEM_BYTES - 4 * 1024 * 1024))


def _cparams(sem, nbytes):
    return pltpu.CompilerParams(dimension_semantics=sem, vmem_limit_bytes=_vmem_limit(nbytes))


def _sigmoid(x):
    return 1.0 / (1.0 + jnp.exp(-x))


def _ada_kernel(c_ref, w_ref, b_ref, o_ref):
    cv = c_ref[...]
    s = cv * _sigmoid(cv)
    o_ref[...] = jnp.dot(s, w_ref[0], preferred_element_type=F32) + b_ref[...]


def _ada(c8, ada_w, ada_b):
    d = c8.shape[1]
    n_out = ada_w.shape[2]
    return pl.pallas_call(
        _ada_kernel,
        grid=(n_out // ADA_TN,),
        in_specs=[
            pl.BlockSpec((MOD_ROWS, d), lambda j: (0, 0)),
            pl.BlockSpec((1, d, ADA_TN), lambda j: (0, 0, j)),
            pl.BlockSpec((1, ADA_TN), lambda j: (0, j)),
        ],
        out_specs=pl.BlockSpec((MOD_ROWS, ADA_TN), lambda j: (0, j)),
        out_shape=jax.ShapeDtypeStruct((MOD_ROWS, n_out), F32),
        compiler_params=_cparams(("arbitrary",), 2 * d * ADA_TN * 4 + (8 << 20)),
        name="ada",
    )(c8, ada_w, ada_b)


def _rms(xh, g):
    ms = jnp.mean(xh * xh, axis=-1, keepdims=True)
    return xh * lax.rsqrt(ms + QK_EPS) * g


def _rope(xn, cos, sin_lo, sin_hi):
    return (xn * cos + pltpu.roll(xn, HEAD_DIM - ROPE_AXIS_DIM // 2, 1) * sin_lo
            + pltpu.roll(xn, ROPE_AXIS_DIM // 2, 1) * sin_hi)


def _inproj_kernel(x_ref, sc_ref, sh_ref, w_ref, qn_ref, kn_ref, cos_ref, slo_ref, shi_ref,
                   u_ref, q_ref, kt_ref, v_ref, *, tiles_per_batch):
    b = pl.program_id(0) // tiles_per_batch
    sc = sc_ref[pl.ds(b, 1), :]
    sh = sh_ref[pl.ds(b, 1), :]
    h = x_ref[...] * (1.0 + sc) + sh
    proj = jnp.dot(h, w_ref[0], preferred_element_type=F32)
    u_ref[...] = proj[:, :Q_OFF]
    cos = cos_ref[...]
    slo = slo_ref[...]
    shi = shi_ref[...]
    qn = qn_ref[...]
    kn = kn_ref[...]
    for hd in range(N_HEADS):
        xh = proj[:, Q_OFF + hd * HEAD_DIM:Q_OFF + (hd + 1) * HEAD_DIM]
        qr = _rope(_rms(xh, qn), cos, slo, shi) * ATTN_SCALE
        q_ref[0, hd] = qr.astype(BF16)
    for hd in range(N_KV_HEADS):
        kh = proj[:, K_OFF + hd * HEAD_DIM:K_OFF + (hd + 1) * HEAD_DIM]
        kr = _rope(_rms(kh, kn), cos, slo, shi)
        kt_ref[0, hd] = kr.T.astype(BF16)
        v_ref[0, hd] = proj[:, V_OFF + hd * HEAD_DIM:V_OFF + (hd + 1) * HEAD_DIM].astype(BF16)


def _inproj(x2, sc1, sh1, w_in, q_norm, k_norm, cos, sin_lo, sin_hi, bsz, n_tok):
    tm = INPROJ_TM
    tpb = n_tok // tm
    d = x2.shape[1]
    est = (2 * tm * d * 4 + d * IN_WIDTH * 4 + 2 * tm * IN_WIDTH * 4 + 2 * tm * IN_WIDTH * 4
           + (8 << 20))
    return pl.pallas_call(
        functools.partial(_inproj_kernel, tiles_per_batch=tpb),
        grid=(bsz * tpb,),
        in_specs=[
            pl.BlockSpec((tm, d), lambda i: (i, 0)),
            pl.BlockSpec((MOD_ROWS, d), lambda i: (0, 0)),
            pl.BlockSpec((MOD_ROWS, d), lambda i: (0, 0)),
            pl.BlockSpec((1, d, IN_WIDTH), lambda i: (0, 0, 0)),
            pl.BlockSpec((1, HEAD_DIM), lambda i: (0, 0)),
            pl.BlockSpec((1, HEAD_DIM), lambda i: (0, 0)),
            pl.BlockSpec((tm, HEAD_DIM), lambda i: (i % tpb, 0)),
            pl.BlockSpec((tm, HEAD_DIM), lambda i: (i % tpb, 0)),
            pl.BlockSpec((tm, HEAD_DIM), lambda i: (i % tpb, 0)),
        ],
        out_specs=[
            pl.BlockSpec((tm, POOL_WIDTH), lambda i: (i, 0)),
            pl.BlockSpec((1, N_HEADS, tm, HEAD_DIM), lambda i: (i // tpb, 0, i % tpb, 0)),
            pl.BlockSpec((1, N_KV_HEADS, HEAD_DIM, tm), lambda i: (i // tpb, 0, 0, i % tpb)),
            pl.BlockSpec((1, N_KV_HEADS, tm, HEAD_DIM), lambda i: (i // tpb, 0, i % tpb, 0)),
        ],
        out_shape=[
            jax.ShapeDtypeStruct((bsz * n_tok, POOL_WIDTH), F32),
            jax.ShapeDtypeStruct((bsz, N_HEADS, n_tok, HEAD_DIM), BF16),
            jax.ShapeDtypeStruct((bsz, N_KV_HEADS, HEAD_DIM, n_tok), BF16),
            jax.ShapeDtypeStruct((bsz, N_KV_HEADS, n_tok, HEAD_DIM), BF16),
        ],
        compiler_params=_cparams(("arbitrary",), est),
        name="inproj",
    )(x2, sc1, sh1, w_in, q_norm, k_norm, cos, sin_lo, sin_hi)


def _ctxkv_kernel(ctx_ref, sc_ref, sh_ref, w_ref, kn_ref, kt_ref, v_ref, *, bsz, n_ctx):
    hc = ctx_ref[...] * (1.0 + sc_ref[2:3, :]) + sh_ref[2:3, :]
    kv = jnp.dot(hc, w_ref[0], preferred_element_type=F32)
    kn = kn_ref[...]
    for b in range(bsz):
        rows = slice(b * n_ctx, (b + 1) * n_ctx)
        for hd in range(N_KV_HEADS):
            kh = kv[rows, hd * HEAD_DIM:(hd + 1) * HEAD_DIM]
            kt_ref[b, hd] = _rms(kh, kn).T.astype(BF16)
            v_ref[b, hd] = kv[rows, KV_WIDTH + hd * HEAD_DIM:KV_WIDTH + (hd + 1) * HEAD_DIM].astype(BF16)


def _ctxkv(ctx2, sc1, sh1, w_in, k_norm, bsz, n_ctx):
    d = ctx2.shape[1]
    kvw = 2 * KV_WIDTH
    return pl.pallas_call(
        functools.partial(_ctxkv_kernel, bsz=bsz, n_ctx=n_ctx),
        grid=(1,),
        in_specs=[
            pl.BlockSpec((bsz * n_ctx, d), lambda i: (0, 0)),
            pl.BlockSpec((MOD_ROWS, d), lambda i: (0, 0)),
            pl.BlockSpec((MOD_ROWS, d), lambda i: (0, 0)),
            pl.BlockSpec((1, d, kvw), lambda i: (0, 0, K_OFF // kvw)),
            pl.BlockSpec((1, HEAD_DIM), lambda i: (0, 0)),
        ],
        out_specs=[
            pl.BlockSpec((bsz, N_KV_HEADS, HEAD_DIM, n_ctx), lambda i: (0, 0, 0, 0)),
            pl.BlockSpec((bsz, N_KV_HEADS, n_ctx, HEAD_DIM), lambda i: (0, 0, 0, 0)),
        ],
        out_shape=[
            jax.ShapeDtypeStruct((bsz, N_KV_HEADS, HEAD_DIM, n_ctx), BF16),
            jax.ShapeDtypeStruct((bsz, N_KV_HEADS, n_ctx, HEAD_DIM), BF16),
        ],
        compiler_params=_cparams(("arbitrary",), 2 * (bsz * n_ctx * d * 4 + d * kvw * 4) + (8 << 20)),
        name="ctxkv",
    )(ctx2, sc1, sh1, w_in, k_norm)


def _pool_kernel(up_ref, uc_ref, un_ref, w_ref, s_ref, o_ref, ext_ref, *, tiles_per_batch, tm, n_tok):
    it = pl.program_id(0) % tiles_per_batch
    ext_ref[0:HALO, :] = jnp.where(it == 0, 0.0, up_ref[...])
    ext_ref[HALO:HALO + tm, :] = uc_ref[...]
    ext_ref[HALO + tm:HALO + tm + HALO, :] = jnp.where(it == tiles_per_batch - 1, 0.0, un_ref[...])
    t = it * tm + lax.broadcasted_iota(jnp.int32, (tm, POOL_GROUP), 0)
    for gi, w in enumerate(POOL_WINDOWS):
        half = w // 2
        cols = slice(gi * POOL_GROUP, (gi + 1) * POOL_GROUP)
        acc = ext_ref[HALO - half:HALO - half + tm, cols]
        for dd in range(-half + 1, half):
            acc = acc + ext_ref[HALO + dd:HALO + dd + tm, cols]
        cnt = (jnp.minimum(t + half, n_tok) - jnp.maximum(t - half, 0)).astype(F32)
        diff = acc / cnt - uc_ref[:, cols]
        mixed = jnp.dot(diff, w_ref[0, gi], preferred_element_type=F32) * s_ref[:, cols]
        o_ref[:, cols] = mixed.astype(BF16)


def _pool(u, pool_w, pool_scale, bsz, n_tok):
    tm = POOL_TM
    tpb = n_tok // tm
    n_rows = bsz * n_tok
    hb = tm // HALO
    last_hblk = n_rows // HALO - 1
    return pl.pallas_call(
        functools.partial(_pool_kernel, tiles_per_batch=tpb, tm=tm, n_tok=n_tok),
        grid=(bsz * tpb,),
        in_specs=[
            pl.BlockSpec((HALO, POOL_WIDTH), lambda i: (jnp.maximum(i * hb - 1, 0), 0)),
            pl.BlockSpec((tm, POOL_WIDTH), lambda i: (i, 0)),
            pl.BlockSpec((HALO, POOL_WIDTH), lambda i: (jnp.minimum((i + 1) * hb, last_hblk), 0)),
            pl.BlockSpec((1, len(POOL_WINDOWS), POOL_GROUP, POOL_GROUP), lambda i: (0, 0, 0, 0)),
            pl.BlockSpec((1, POOL_WIDTH), lambda i: (0, 0)),
        ],
        out_specs=pl.BlockSpec((tm, POOL_WIDTH), lambda i: (i, 0)),
        out_shape=jax.ShapeDtypeStruct((n_rows, POOL_WIDTH), BF16),
        scratch_shapes=[pltpu.VMEM((tm + 2 * HALO, POOL_WIDTH), F32)],
        compiler_params=_cparams(("arbitrary",), 6 * tm * POOL_WIDTH * 4 + (8 << 20)),
        name="pool",
    )(u, u, u, pool_w, pool_scale)


def _attn_kernel(q_ref, kt_ref, v_ref, kct_ref, vc_ref, o_ref, *, tq, tk, n_tok):
    rows = Q_PER_KV * tq
    q = q_ref[0].reshape(rows, HEAD_DIM)

    def step(carry, kt_c, v_c):
        m, l, acc = carry
        s = jnp.dot(q, kt_c, preferred_element_type=F32)
        m_new = jnp.maximum(m, jnp.max(s, axis=-1, keepdims=True))
        alpha = jnp.exp(m - m_new)
        p = jnp.exp(s - m_new)
        l = alpha * l + jnp.sum(p, axis=-1, keepdims=True)
        acc = alpha * acc + jnp.dot(p.astype(BF16), v_c, preferred_element_type=F32)
        return m_new, l, acc

    carry = (jnp.full((rows, 1), -jnp.inf, F32), jnp.zeros((rows, 1), F32),
             jnp.zeros((rows, HEAD_DIM), F32))
    carry = step(carry, kct_ref[0, 0], vc_ref[0, 0])
    for j in range(n_tok // tk):
        carry = step(carry, kt_ref[0, 0, :, j * tk:(j + 1) * tk], v_ref[0, 0, j * tk:(j + 1) * tk, :])
    _, l, acc = carry
    out = acc / l
    for g in range(Q_PER_KV):
        o_ref[:, g * HEAD_DIM:(g + 1) * HEAD_DIM] = out[g * tq:(g + 1) * tq].astype(BF16)


def _attn(q, kt, v, kct, vc, bsz, n_tok, n_ctx):
    tq, tk = ATTN_TQ, ATTN_TK
    nq = n_tok // tq
    gw = Q_PER_KV * HEAD_DIM
    return pl.pallas_call(
        functools.partial(_attn_kernel, tq=tq, tk=tk, n_tok=n_tok),
        grid=(bsz, N_KV_HEADS, nq),
        in_specs=[
            pl.BlockSpec((1, Q_PER_KV, tq, HEAD_DIM), lambda b, h, i: (b, h, i, 0)),
            pl.BlockSpec((1, 1, HEAD_DIM, n_tok), lambda b, h, i: (b, h, 0, 0)),
            pl.BlockSpec((1, 1, n_tok, HEAD_DIM), lambda b, h, i: (b, h, 0, 0)),
            pl.BlockSpec((1, 1, HEAD_DIM, n_ctx), lambda b, h, i: (b, h, 0, 0)),
            pl.BlockSpec((1, 1, n_ctx, HEAD_DIM), lambda b, h, i: (b, h, 0, 0)),
        ],
        out_specs=pl.BlockSpec((tq, gw), lambda b, h, i: (b * nq + i, h)),
        out_shape=jax.ShapeDtypeStruct((bsz * n_tok, ATTN_WIDTH), BF16),
        compiler_params=_cparams(("arbitrary", "arbitrary", "arbitrary"), 40 << 20),
        name="attn",
    )(q, kt, v, kct, vc)


def _layer_norm(r, g, b):
    mu = jnp.mean(r, axis=-1, keepdims=True)
    rc = r - mu
    var = jnp.mean(rc * rc, axis=-1, keepdims=True)
    return rc * lax.rsqrt(var + LN_EPS) * g + b


def _split_bf16(x):
    hi = x.astype(BF16)
    lo = (x - hi.astype(F32)).astype(BF16)
    return hi, lo


def _outproj_kernel(pool_ref, attn_ref, x_ref, w_ref, g1_ref, sc_ref, sh_ref, lg_ref, lb_ref, rw_ref,
                    acc_ref, h_ref, *, tiles_per_batch):
    b = pl.program_id(0) // tiles_per_batch
    y = (jnp.dot(pool_ref[...].astype(F32), w_ref[0, :POOL_WIDTH, :], preferred_element_type=F32)
         + jnp.dot(attn_ref[...].astype(F32), w_ref[0, POOL_WIDTH:, :], preferred_element_type=F32))
    r = ALPHA * x_ref[...] + g1_ref[pl.ds(b, 1), :] * y
    xm = _layer_norm(r, lg_ref[...], lb_ref[...])
    acc_ref[...] = ALPHA * xm
    h2 = xm * (1.0 + sc_ref[pl.ds(b, 1), :]) + sh_ref[pl.ds(b, 1), :]
    d = h2.shape[1]
    h_ref[:, :d] = h2
    hh, hl = _split_bf16(h2)
    rh, rl = _split_bf16(rw_ref[...])
    logits = (jnp.dot(hh, rh, preferred_element_type=F32) + jnp.dot(hl, rh, preferred_element_type=F32)
              + jnp.dot(hh, rl, preferred_element_type=F32))
    valid = lax.broadcasted_iota(jnp.int32, logits.shape, 1) < N_EXPERTS
    lg = jnp.where(valid, logits, -jnp.inf)
    ex = jnp.exp(lg - jnp.max(lg, axis=-1, keepdims=True))
    h_ref[:, d:] = ex / jnp.sum(ex, axis=-1, keepdims=True)


def _outproj(pool, attn, x2, w_out, g1, sc2, sh2, ln_g, ln_b, rw_pad, bsz, n_tok):
    tm = OUTPROJ_TM
    tpb = n_tok // tm
    d = x2.shape[1]
    n_rows = bsz * n_tok
    est = d * d * 4 + 2 * tm * d * 4 * 4 + 6 * tm * d * 4 + (8 << 20)
    return pl.pallas_call(
        functools.partial(_outproj_kernel, tiles_per_batch=tpb),
        grid=(bsz * tpb,),
        in_specs=[
            pl.BlockSpec((tm, POOL_WIDTH), lambda i: (i, 0)),
            pl.BlockSpec((tm, ATTN_WIDTH), lambda i: (i, 0)),
            pl.BlockSpec((tm, d), lambda i: (i, 0)),
            pl.BlockSpec((1, d, d), lambda i: (0, 0, 0)),
            pl.BlockSpec((MOD_ROWS, d), lambda i: (0, 0)),
            pl.BlockSpec((MOD_ROWS, d), lambda i: (0, 0)),
            pl.BlockSpec((MOD_ROWS, d), lambda i: (0, 0)),
            pl.BlockSpec((1, d), lambda i: (0, 0)),
            pl.BlockSpec((1, d), lambda i: (0, 0)),
            pl.BlockSpec((d, LANES), lambda i: (0, 0)),
        ],
        out_specs=[
            pl.BlockSpec((tm, d), lambda i: (i, 0)),
            pl.BlockSpec((tm, d + LANES), lambda i: (i, 0)),
        ],
        out_shape=[
            jax.ShapeDtypeStruct((n_rows, d), F32),
            jax.ShapeDtypeStruct((n_rows, d + LANES), F32),
        ],
        compiler_params=_cparams(("arbitrary",), est),
        name="outproj",
    )(pool, attn, x2, w_out, g1, sc2, sh2, ln_g, ln_b, rw_pad)


def _prefix_incl(flags_bf16, out_ref, n_tok):
    blk = PREFIX_BLK
    tri = (lax.broadcasted_iota(jnp.int32, (blk, blk), 0)
           >= lax.broadcasted_iota(jnp.int32, (blk, blk), 1)).astype(BF16)
    carry = jnp.zeros((1, LANES), F32)
    for k in range(n_tok // blk):
        p = jnp.dot(tri, flags_bf16[k * blk:(k + 1) * blk], preferred_element_type=F32) + carry
        out_ref[k * blk:(k + 1) * blk, :] = p
        carry = p[blk - 1:blk, :]


def _topk_kernel(aff_ref, idx_ref, cnt_ref, *, n_tok, cap):
    lane = lax.broadcasted_iota(jnp.int32, (1, LANES), 1)
    bits = jnp.where(lane < N_EXPERTS, pltpu.bitcast(aff_ref[...], jnp.int32), -1)

    def search(_, carry):
        lo, hi = carry
        mid = lo + ((hi - lo + 1) >> 1)
        n_ge = jnp.sum((bits >= mid).astype(jnp.int32), axis=0, keepdims=True)
        ok = n_ge >= cap
        return jnp.where(ok, mid, lo), jnp.where(ok, hi, mid - 1)

    lo0 = jnp.zeros((1, LANES), jnp.int32)
    hi0 = jnp.full((1, LANES), 0x7F800000, jnp.int32)
    thr, _ = lax.fori_loop(0, 32, search, (lo0, hi0))

    gt = bits > thr
    eq = bits == thr
    need = (cap - jnp.sum(gt.astype(jnp.int32), axis=0, keepdims=True)).astype(F32)
    _prefix_incl(jnp.where(eq, 1.0, 0.0).astype(BF16), cnt_ref, n_tok)
    sel = gt | (eq & (cnt_ref[...] <= need))
    _prefix_incl(jnp.where(sel, 1.0, 0.0).astype(BF16), cnt_ref, n_tok)

    slot = lax.broadcasted_iota(jnp.int32, (1, cap), 1).astype(F32)
    blk = PREFIX_BLK
    for e in range(N_EXPERTS):
        def count(k, acc, e=e):
            r0 = pl.multiple_of(k * blk, blk)
            c = cnt_ref[pl.ds(r0, blk), e:e + 1]
            return acc + jnp.sum((c <= slot).astype(F32), axis=0, keepdims=True)

        tok = lax.fori_loop(0, n_tok // blk, count, jnp.zeros((1, cap), F32))
        idx_ref[0, e:e + 1, :] = tok.astype(jnp.int32)


def _topk(h2ext, bsz, n_tok, cap):
    aff_blk = h2ext.shape[1] // LANES - 1
    return pl.pallas_call(
        functools.partial(_topk_kernel, n_tok=n_tok, cap=cap),
        grid=(bsz,),
        in_specs=[pl.BlockSpec((n_tok, LANES), lambda b: (b, aff_blk))],
        out_specs=pl.BlockSpec((1, N_EXPERTS, cap), lambda b: (b, 0, 0)),
        out_shape=jax.ShapeDtypeStruct((bsz, N_EXPERTS, cap), jnp.int32),
        scratch_shapes=[pltpu.VMEM((n_tok, LANES), F32)],
        compiler_params=_cparams(("arbitrary",), 32 << 20),
        name="topk",
    )(h2ext)


def _row_copies(idx_ref, n_rows, make):
    def body(j, _):
        make(idx_ref[0, 0, j], j).start()
        return 0

    lax.fori_loop(0, n_rows, body, 0)


def _moe_kernel(idx_ref, h_hbm, acc_in_hbm, wg_ref, wu_ref, wd_ref, g2_ref, acc_hbm,
                xe_ref, y_ref, ab_ref, sem, *, n_f, n_rows, d, cap):
    del acc_in_hbm
    e = pl.program_id(0)
    f = pl.program_id(1)

    @pl.when(f == 0)
    def _():
        _row_copies(idx_ref, n_rows, lambda r, j: pltpu.make_async_copy(
            h_hbm.at[pl.ds(r, 1), :], xe_ref.at[pl.ds(j, 1), :], sem.at[0]))
        pltpu.make_async_copy(h_hbm.at[pl.ds(0, n_rows), :], xe_ref, sem.at[0]).wait()
        y_ref[...] = jnp.zeros_like(y_ref)

    xe = xe_ref[:, :d]
    a = jnp.dot(xe, wg_ref[0, 0], preferred_element_type=F32)
    b = jnp.dot(xe, wu_ref[0, 0], preferred_element_type=F32)
    hm = a * _sigmoid(a) * b
    y_ref[...] += jnp.dot(hm, wd_ref[0, 0], preferred_element_type=F32)

    @pl.when(f == n_f - 1)
    def _():
        _row_copies(idx_ref, n_rows, lambda r, j: pltpu.make_async_copy(
            acc_hbm.at[pl.ds(r, 1), :], ab_ref.at[pl.ds(j, 1), :], sem.at[1]))
        pltpu.make_async_copy(acc_hbm.at[pl.ds(0, n_rows), :], ab_ref, sem.at[1]).wait()
        lane = lax.broadcasted_iota(jnp.int32, (n_rows, LANES), 1)
        gate = jnp.sum(jnp.where(lane == e, xe_ref[:, d:], 0.0), axis=-1, keepdims=True)
        for bi in range(n_rows // cap):
            rows = slice(bi * cap, (bi + 1) * cap)
            ab_ref[rows, :] += (gate[rows] * g2_ref[bi:bi + 1, :]) * y_ref[rows, :]
        _row_copies(idx_ref, n_rows, lambda r, j: pltpu.make_async_copy(
            ab_ref.at[pl.ds(j, 1), :], acc_hbm.at[pl.ds(r, 1), :], sem.at[2]))
        pltpu.make_async_copy(ab_ref, acc_hbm.at[pl.ds(0, n_rows), :], sem.at[2]).wait()


def _moe(idx_rows, h2ext, acc0, w_gate, w_up, w_down, g2, cap):
    d = acc0.shape[1]
    n_rows = idx_rows.shape[2]
    ff = w_gate.shape[3]
    tf = MOE_TF
    n_f = ff // tf
    est = (2 * 3 * d * tf * 4 + n_rows * h2ext.shape[1] * 4 + 2 * n_rows * d * 4
           + 6 * n_rows * tf * 4 + (8 << 20))
    return pl.pallas_call(
        functools.partial(_moe_kernel, n_f=n_f, n_rows=n_rows, d=d, cap=cap),
        grid=(N_EXPERTS, n_f),
        in_specs=[
            pl.BlockSpec((1, 1, n_rows), lambda e, f: (e, 0, 0), memory_space=pltpu.SMEM),
            pl.BlockSpec(memory_space=pl.ANY),
            pl.BlockSpec(memory_space=pl.ANY),
            pl.BlockSpec((1, 1, d, tf), lambda e, f: (0, e, 0, f)),
            pl.BlockSpec((1, 1, d, tf), lambda e, f: (0, e, 0, f)),
            pl.BlockSpec((1, 1, tf, d), lambda e, f: (0, e, f, 0)),
            pl.BlockSpec((MOD_ROWS, d), lambda e, f: (0, 0)),
        ],
        out_specs=pl.BlockSpec(memory_space=pl.ANY),
        out_shape=jax.ShapeDtypeStruct(acc0.shape, F32),
        scratch_shapes=[
            pltpu.VMEM((n_rows, h2ext.shape[1]), F32),
            pltpu.VMEM((n_rows, d), F32),
            pltpu.VMEM((n_rows, d), F32),
            pltpu.SemaphoreType.DMA((3,)),
        ],
        input_output_aliases={2: 0},
        compiler_params=_cparams(("arbitrary", "arbitrary"), est),
        name="moe",
    )(idx_rows, h2ext, acc0, w_gate, w_up, w_down, g2)


def _final_kernel(a_ref, g_ref, b_ref, o_ref):
    o_ref[...] = _layer_norm(a_ref[...], g_ref[...], b_ref[...])


def _final(acc, ln_g, ln_b):
    n_rows, d = acc.shape
    tm = FINAL_TM
    return pl.pallas_call(
        _final_kernel,
        grid=(n_rows // tm,),
        in_specs=[
            pl.BlockSpec((tm, d), lambda i: (i, 0)),
            pl.BlockSpec((1, d), lambda i: (0, 0)),
            pl.BlockSpec((1, d), lambda i: (0, 0)),
        ],
        out_specs=pl.BlockSpec((tm, d), lambda i: (i, 0)),
        out_shape=jax.ShapeDtypeStruct((n_rows, d), F32),
        compiler_params=_cparams(("arbitrary",), 8 * tm * d * 4 + (8 << 20)),
        name="final_ln",
    )(acc, ln_g, ln_b)


def _rope_tables(n_tok):
    rows = n_tok // GRID_W
    row = jnp.broadcast_to(jnp.arange(rows, dtype=F32)[:, None], (rows, GRID_W)).reshape(-1)
    col = jnp.broadcast_to(jnp.arange(GRID_W, dtype=F32)[None, :], (rows, GRID_W)).reshape(-1)
    inv_freq = ROPE_THETA ** (-jnp.arange(0, ROPE_AXIS_DIM, 2, dtype=F32) / ROPE_AXIS_DIM)
    ang_r = row[:, None] * inv_freq[None, :]
    ang_c = col[:, None] * inv_freq[None, :]
    ang = jnp.concatenate([ang_r, ang_r, ang_c, ang_c], axis=-1)
    cos, sin = jnp.cos(ang), jnp.sin(ang)
    low = (np.arange(HEAD_DIM) % ROPE_AXIS_DIM) < ROPE_AXIS_DIM // 2
    sin_lo = jnp.where(low[None, :], -sin, 0.0)
    sin_hi = jnp.where(low[None, :], 0.0, sin)
    return cos, sin_lo, sin_hi


def kernel(x, c, ctx, c_ctx, ada_w, ada_b, w_in, q_norm, k_norm, pool_w, pool_scale, w_out,
           ln1_g, ln1_b, router_w, w_gate, w_up, w_down, ln2_g, ln2_b):
    bsz, n_tok, d = x.shape
    n_ctx = ctx.shape[1]
    assert ada_w.shape[0] == DEPTH and d == D_MODEL and bsz + 1 <= MOD_ROWS
    cap = CAPACITY_FACTOR * n_tok // N_EXPERTS

    c8 = jnp.concatenate([c, c_ctx[None, :], jnp.zeros((MOD_ROWS - bsz - 1, d), F32)], axis=0)
    mod = _ada(c8, ada_w, ada_b)
    sh1, sc1, g1, sh2, sc2, g2 = [mod[:, k * d:(k + 1) * d] for k in range(6)]

    x2 = x.reshape(bsz * n_tok, d)
    cos, sin_lo, sin_hi = _rope_tables(n_tok)
    u, q, kt, v = _inproj(x2, sc1, sh1, w_in, q_norm, k_norm, cos, sin_lo, sin_hi, bsz, n_tok)
    kct, vc = _ctxkv(ctx.reshape(bsz * n_ctx, d), sc1, sh1, w_in, k_norm, bsz, n_ctx)

    pool = _pool(u, pool_w, pool_scale, bsz, n_tok)
    attn = _attn(q, kt, v, kct, vc, bsz, n_tok, n_ctx)

    rw_pad = jnp.pad(router_w[0], ((0, 0), (0, LANES - N_EXPERTS)))
    acc0, h2ext = _outproj(pool, attn, x2, w_out, g1, sc2, sh2, ln1_g, ln1_b, rw_pad, bsz, n_tok)

    idx = _topk(h2ext, bsz, n_tok, cap)
    idx_rows = (idx + (jnp.arange(bsz, dtype=jnp.int32) * n_tok)[:, None, None])
    idx_rows = idx_rows.transpose(1, 0, 2).reshape(N_EXPERTS, 1, bsz * cap)

    acc = _moe(idx_rows, h2ext, acc0, w_gate, w_up, w_down, g2, cap)
    out = _final(acc, ln2_g, ln2_b)
    return out.reshape(bsz, n_tok, d)
```

```python
import functools

import jax
import jax.numpy as jnp
import numpy as np
from jax import lax
from jax.experimental import pallas as pl
from jax.experimental.pallas import tpu as pltpu

F32 = jnp.float32
BF16 = jnp.bfloat16

D_MODEL = 2048
GRID_W = 64
HEAD_DIM = 128
ATTN_WIDTH = D_MODEL // 2
N_HEADS = ATTN_WIDTH // HEAD_DIM
N_KV_HEADS = 2
Q_PER_KV = N_HEADS // N_KV_HEADS
KV_WIDTH = N_KV_HEADS * HEAD_DIM
POOL_WIDTH = D_MODEL - ATTN_WIDTH
POOL_WINDOWS = (2, 4, 8, 16)
POOL_GROUP = POOL_WIDTH // len(POOL_WINDOWS)
Q_OFF = POOL_WIDTH
K_OFF = Q_OFF + ATTN_WIDTH
V_OFF = K_OFF + KV_WIDTH
IN_WIDTH = V_OFF + KV_WIDTH
ROPE_AXIS_DIM = HEAD_DIM // 2
ROPE_THETA = 10000.0
ATTN_SCALE = HEAD_DIM ** -0.5
N_EXPERTS = 16
CAPACITY_FACTOR = 2
LN_EPS = 1e-6
QK_EPS = 1e-6
DEPTH = 1
ALPHA = (2.0 * DEPTH) ** 0.25

LANES = 128
SUBLANES = 8
V7X_VMEM_BYTES = 64 * 1024 * 1024
MOD_ROWS = SUBLANES
HALO = SUBLANES

ADA_TN = 1024
SUB_ROWS = 256
INPROJ_TM = 512
POOL_TM = 512
ATTN_TQ = 256
ATTN_TK = 512
OUTPROJ_TM = 512
MOE_TF = 256
MOE_XGATHER_STEPS = 16
MOE_SCATTER_STEPS = 8
FINAL_TM = 512
PREFIX_BLK = 256
COUNT_CHAINS = 8


def _vmem_limit(nbytes):
    return int(min(nbytes, V7X_VMEM_BYTES - 4 * 1024 * 1024))


def _cparams(sem, nbytes):
    return pltpu.CompilerParams(dimension_semantics=sem, vmem_limit_bytes=_vmem_limit(nbytes))


def _sigmoid(x):
    return 1.0 / (1.0 + jnp.exp(-x))


def _ada_kernel(c_ref, w_ref, b_ref, o_ref):
    cv = c_ref[...]
    s = cv * _sigmoid(cv)
    o_ref[...] = jnp.dot(s, w_ref[0], preferred_element_type=F32) + b_ref[...]


def _ada(c8, ada_w, ada_b):
    d = c8.shape[1]
    n_out = ada_w.shape[2]
    return pl.pallas_call(
        _ada_kernel,
        grid=(n_out // ADA_TN,),
        in_specs=[
            pl.BlockSpec((MOD_ROWS, d), lambda j: (0, 0)),
            pl.BlockSpec((1, d, ADA_TN), lambda j: (0, 0, j)),
            pl.BlockSpec((1, ADA_TN), lambda j: (0, j)),
        ],
        out_specs=pl.BlockSpec((MOD_ROWS, ADA_TN), lambda j: (0, j)),
        out_shape=jax.ShapeDtypeStruct((MOD_ROWS, n_out), F32),
        compiler_params=_cparams(("arbitrary",), 2 * d * ADA_TN * 4 + (8 << 20)),
        name="ada",
    )(c8, ada_w, ada_b)


def _rms(xh, g):
    ms = jnp.mean(xh * xh, axis=-1, keepdims=True)
    return xh * lax.rsqrt(ms + QK_EPS) * g


def _rope(xn, cos, sin_lo, sin_hi):
    return (xn * cos + pltpu.roll(xn, HEAD_DIM - ROPE_AXIS_DIM // 2, 1) * sin_lo
            + pltpu.roll(xn, ROPE_AXIS_DIM // 2, 1) * sin_hi)


def _inproj_kernel(x_ref, sc_ref, sh_ref, w_ref, qn_ref, kn_ref, cos_ref, slo_ref, shi_ref,
                   u_ref, q_ref, kt_ref, v_ref, *, tiles_per_batch, sub):
    b = pl.program_id(0) // tiles_per_batch
    sc = 1.0 + sc_ref[pl.ds(b, 1), :]
    sh = sh_ref[pl.ds(b, 1), :]
    qn = qn_ref[...]
    kn = kn_ref[...]
    for r0 in range(0, x_ref.shape[0], sub):
        rows = slice(r0, r0 + sub)
        h = x_ref[rows, :] * sc + sh
        proj = jnp.dot(h, w_ref[0], preferred_element_type=F32)
        u_ref[rows, :] = proj[:, :Q_OFF]
        cos = cos_ref[rows, :]
        slo = slo_ref[rows, :]
        shi = shi_ref[rows, :]
        for hd in range(N_HEADS):
            xh = proj[:, Q_OFF + hd * HEAD_DIM:Q_OFF + (hd + 1) * HEAD_DIM]
            qr = _rope(_rms(xh, qn), cos, slo, shi) * ATTN_SCALE
            q_ref[0, hd, rows, :] = qr.astype(BF16)
        for hd in range(N_KV_HEADS):
            kh = proj[:, K_OFF + hd * HEAD_DIM:K_OFF + (hd + 1) * HEAD_DIM]
            kr = _rope(_rms(kh, kn), cos, slo, shi)
            kt_ref[0, hd, :, rows] = kr.T.astype(BF16)
            v_ref[0, hd, rows, :] = proj[:, V_OFF + hd * HEAD_DIM:V_OFF + (hd + 1) * HEAD_DIM].astype(BF16)


def _inproj(x2, sc1, sh1, w_in, q_norm, k_norm, cos, sin_lo, sin_hi, bsz, n_tok):
    tm = INPROJ_TM
    tpb = n_tok // tm
    d = x2.shape[1]
    est = (2 * tm * d * 4 + d * IN_WIDTH * 4 + 2 * tm * IN_WIDTH * 4 + 2 * tm * IN_WIDTH * 4
           + (8 << 20))
    return pl.pallas_call(
        functools.partial(_inproj_kernel, tiles_per_batch=tpb, sub=SUB_ROWS),
        grid=(bsz * tpb,),
        in_specs=[
            pl.BlockSpec((tm, d), lambda i: (i, 0)),
            pl.BlockSpec((MOD_ROWS, d), lambda i: (0, 0)),
            pl.BlockSpec((MOD_ROWS, d), lambda i: (0, 0)),
            pl.BlockSpec((1, d, IN_WIDTH), lambda i: (0, 0, 0)),
            pl.BlockSpec((1, HEAD_DIM), lambda i: (0, 0)),
            pl.BlockSpec((1, HEAD_DIM), lambda i: (0, 0)),
            pl.BlockSpec((tm, HEAD_DIM), lambda i: (i % tpb, 0)),
            pl.BlockSpec((tm, HEAD_DIM), lambda i: (i % tpb, 0)),
            pl.BlockSpec((tm, HEAD_DIM), lambda i: (i % tpb, 0)),
        ],
        out_specs=[
            pl.BlockSpec((tm, POOL_WIDTH), lambda i: (i, 0)),
            pl.BlockSpec((1, N_HEADS, tm, HEAD_DIM), lambda i: (i // tpb, 0, i % tpb, 0)),
            pl.BlockSpec((1, N_KV_HEADS, HEAD_DIM, tm), lambda i: (i // tpb, 0, 0, i % tpb)),
            pl.BlockSpec((1, N_KV_HEADS, tm, HEAD_DIM), lambda i: (i // tpb, 0, i % tpb, 0)),
        ],
        out_shape=[
            jax.ShapeDtypeStruct((bsz * n_tok, POOL_WIDTH), F32),
            jax.ShapeDtypeStruct((bsz, N_HEADS, n_tok, HEAD_DIM), BF16),
            jax.ShapeDtypeStruct((bsz, N_KV_HEADS, HEAD_DIM, n_tok), BF16),
            jax.ShapeDtypeStruct((bsz, N_KV_HEADS, n_tok, HEAD_DIM), BF16),
        ],
        compiler_params=_cparams(("arbitrary",), est),
        name="inproj",
    )(x2, sc1, sh1, w_in, q_norm, k_norm, cos, sin_lo, sin_hi)


def _ctxkv_kernel(ctx_ref, sc_ref, sh_ref, w_ref, kn_ref, kt_ref, v_ref, *, bsz, n_ctx):
    hc = ctx_ref[...] * (1.0 + sc_ref[2:3, :]) + sh_ref[2:3, :]
    kv = jnp.dot(hc, w_ref[0], preferred_element_type=F32)
    kn = kn_ref[...]
    for b in range(bsz):
        rows = slice(b * n_ctx, (b + 1) * n_ctx)
        for hd in range(N_KV_HEADS):
            kh = kv[rows, hd * HEAD_DIM:(hd + 1) * HEAD_DIM]
            kt_ref[b, hd] = _rms(kh, kn).T.astype(BF16)
            v_ref[b, hd] = kv[rows, KV_WIDTH + hd * HEAD_DIM:KV_WIDTH + (hd + 1) * HEAD_DIM].astype(BF16)


def _ctxkv(ctx2, sc1, sh1, w_in, k_norm, bsz, n_ctx):
    d = ctx2.shape[1]
    kvw = 2 * KV_WIDTH
    return pl.pallas_call(
        functools.partial(_ctxkv_kernel, bsz=bsz, n_ctx=n_ctx),
        grid=(1,),
        in_specs=[
            pl.BlockSpec((bsz * n_ctx, d), lambda i: (0, 0)),
            pl.BlockSpec((MOD_ROWS, d), lambda i: (0, 0)),
            pl.BlockSpec((MOD_ROWS, d), lambda i: (0, 0)),
            pl.BlockSpec((1, d, kvw), lambda i: (0, 0, K_OFF // kvw)),
            pl.BlockSpec((1, HEAD_DIM), lambda i: (0, 0)),
        ],
        out_specs=[
            pl.BlockSpec((bsz, N_KV_HEADS, HEAD_DIM, n_ctx), lambda i: (0, 0, 0, 0)),
            pl.BlockSpec((bsz, N_KV_HEADS, n_ctx, HEAD_DIM), lambda i: (0, 0, 0, 0)),
        ],
        out_shape=[
            jax.ShapeDtypeStruct((bsz, N_KV_HEADS, HEAD_DIM, n_ctx), BF16),
            jax.ShapeDtypeStruct((bsz, N_KV_HEADS, n_ctx, HEAD_DIM), BF16),
        ],
        compiler_params=_cparams(("arbitrary",), 2 * (bsz * n_ctx * d * 4 + d * kvw * 4) + (8 << 20)),
        name="ctxkv",
    )(ctx2, sc1, sh1, w_in, k_norm)


def _pool_kernel(up_ref, uc_ref, un_ref, w_ref, s_ref, o_ref, ext_ref, *, tiles_per_batch, tm, n_tok):
    it = pl.program_id(0) % tiles_per_batch
    ext_ref[0:HALO, :] = jnp.where(it == 0, 0.0, up_ref[...])
    ext_ref[HALO:HALO + tm, :] = uc_ref[...]
    ext_ref[HALO + tm:HALO + tm + HALO, :] = jnp.where(it == tiles_per_batch - 1, 0.0, un_ref[...])
    t = it * tm + lax.broadcasted_iota(jnp.int32, (tm, POOL_GROUP), 0)
    for gi, w in enumerate(POOL_WINDOWS):
        half = w // 2
        cols = slice(gi * POOL_GROUP, (gi + 1) * POOL_GROUP)
        acc = ext_ref[HALO - half:HALO - half + tm, cols]
        for dd in range(-half + 1, half):
            acc = acc + ext_ref[HALO + dd:HALO + dd + tm, cols]
        cnt = (jnp.minimum(t + half, n_tok) - jnp.maximum(t - half, 0)).astype(F32)
        diff = acc / cnt - uc_ref[:, cols]
        mixed = jnp.dot(diff, w_ref[0, gi], preferred_element_type=F32) * s_ref[:, cols]
        o_ref[:, cols] = mixed.astype(BF16)


def _pool(u, pool_w, pool_scale, bsz, n_tok):
    tm = POOL_TM
    tpb = n_tok // tm
    n_rows = bsz * n_tok
    hb = tm // HALO
    last_hblk = n_rows // HALO - 1
    return pl.pallas_call(
        functools.partial(_pool_kernel, tiles_per_batch=tpb, tm=tm, n_tok=n_tok),
        grid=(bsz * tpb,),
        in_specs=[
            pl.BlockSpec((HALO, POOL_WIDTH), lambda i: (jnp.maximum(i * hb - 1, 0), 0)),
            pl.BlockSpec((tm, POOL_WIDTH), lambda i: (i, 0)),
            pl.BlockSpec((HALO, POOL_WIDTH), lambda i: (jnp.minimum((i + 1) * hb, last_hblk), 0)),
            pl.BlockSpec((1, len(POOL_WINDOWS), POOL_GROUP, POOL_GROUP), lambda i: (0, 0, 0, 0)),
            pl.BlockSpec((1, POOL_WIDTH), lambda i: (0, 0)),
        ],
        out_specs=pl.BlockSpec((tm, POOL_WIDTH), lambda i: (i, 0)),
        out_shape=jax.ShapeDtypeStruct((n_rows, POOL_WIDTH), BF16),
        scratch_shapes=[pltpu.VMEM((tm + 2 * HALO, POOL_WIDTH), F32)],
        compiler_params=_cparams(("arbitrary",), 6 * tm * POOL_WIDTH * 4 + (8 << 20)),
        name="pool",
    )(u, u, u, pool_w, pool_scale)


def _attn_kernel(q_ref, kt_ref, v_ref, kct_ref, vc_ref, o_ref, *, tq, tk, n_tok):
    rows = Q_PER_KV * tq
    q = q_ref[0].reshape(rows, HEAD_DIM)

    def step(carry, kt_c, v_c):
        m, l, acc = carry
        s = jnp.dot(q, kt_c, preferred_element_type=F32)
        m_new = jnp.maximum(m, jnp.max(s, axis=-1, keepdims=True))
        alpha = jnp.exp(m - m_new)
        p = jnp.exp(s - m_new)
        l = alpha * l + jnp.sum(p, axis=-1, keepdims=True)
        acc = alpha * acc + jnp.dot(p.astype(BF16), v_c, preferred_element_type=F32)
        return m_new, l, acc

    carry = (jnp.full((rows, 1), -jnp.inf, F32), jnp.zeros((rows, 1), F32),
             jnp.zeros((rows, HEAD_DIM), F32))
    carry = step(carry, kct_ref[0, 0], vc_ref[0, 0])
    for j in range(n_tok // tk):
        carry = step(carry, kt_ref[0, 0, :, j * tk:(j + 1) * tk], v_ref[0, 0, j * tk:(j + 1) * tk, :])
    _, l, acc = carry
    out = acc / l
    for g in range(Q_PER_KV):
        o_ref[:, g * HEAD_DIM:(g + 1) * HEAD_DIM] = out[g * tq:(g + 1) * tq].astype(BF16)


def _attn(q, kt, v, kct, vc, bsz, n_tok, n_ctx):
    tq, tk = ATTN_TQ, ATTN_TK
    nq = n_tok // tq
    gw = Q_PER_KV * HEAD_DIM
    return pl.pallas_call(
        functools.partial(_attn_kernel, tq=tq, tk=tk, n_tok=n_tok),
        grid=(bsz, N_KV_HEADS, nq),
        in_specs=[
            pl.BlockSpec((1, Q_PER_KV, tq, HEAD_DIM), lambda b, h, i: (b, h, i, 0)),
            pl.BlockSpec((1, 1, HEAD_DIM, n_tok), lambda b, h, i: (b, h, 0, 0)),
            pl.BlockSpec((1, 1, n_tok, HEAD_DIM), lambda b, h, i: (b, h, 0, 0)),
            pl.BlockSpec((1, 1, HEAD_DIM, n_ctx), lambda b, h, i: (b, h, 0, 0)),
            pl.BlockSpec((1, 1, n_ctx, HEAD_DIM), lambda b, h, i: (b, h, 0, 0)),
        ],
        out_specs=pl.BlockSpec((tq, gw), lambda b, h, i: (b * nq + i, h)),
        out_shape=jax.ShapeDtypeStruct((bsz * n_tok, ATTN_WIDTH), BF16),
        compiler_params=_cparams(("arbitrary", "arbitrary", "arbitrary"), 40 << 20),
        name="attn",
    )(q, kt, v, kct, vc)


def _layer_norm(r, g, b):
    mu = jnp.mean(r, axis=-1, keepdims=True)
    rc = r - mu
    var = jnp.mean(rc * rc, axis=-1, keepdims=True)
    return rc * lax.rsqrt(var + LN_EPS) * g + b


def _split_bf16(x):
    hi = x.astype(BF16)
    lo = (x - hi.astype(F32)).astype(BF16)
    return hi, lo


def _outproj_kernel(pool_ref, attn_ref, x_ref, w_ref, g1_ref, sc_ref, sh_ref, lg_ref, lb_ref, rw_ref,
                    acc_ref, h_ref, *, tiles_per_batch, sub):
    b = pl.program_id(0) // tiles_per_batch
    g1 = g1_ref[pl.ds(b, 1), :]
    sc = 1.0 + sc_ref[pl.ds(b, 1), :]
    sh = sh_ref[pl.ds(b, 1), :]
    rh, rl = _split_bf16(rw_ref[...])
    d = x_ref.shape[1]
    for r0 in range(0, x_ref.shape[0], sub):
        rows = slice(r0, r0 + sub)
        y = (jnp.dot(pool_ref[rows, :].astype(F32), w_ref[0, :POOL_WIDTH, :], preferred_element_type=F32)
             + jnp.dot(attn_ref[rows, :].astype(F32), w_ref[0, POOL_WIDTH:, :], preferred_element_type=F32))
        r = ALPHA * x_ref[rows, :] + g1 * y
        xm = _layer_norm(r, lg_ref[...], lb_ref[...])
        acc_ref[rows, :] = ALPHA * xm
        h2 = xm * sc + sh
        h_ref[rows, :d] = h2
        hh, hl = _split_bf16(h2)
        logits = (jnp.dot(hh, rh, preferred_element_type=F32) + jnp.dot(hl, rh, preferred_element_type=F32)
                  + jnp.dot(hh, rl, preferred_element_type=F32))
        valid = lax.broadcasted_iota(jnp.int32, logits.shape, 1) < N_EXPERTS
        lg = jnp.where(valid, logits, -jnp.inf)
        ex = jnp.exp(lg - jnp.max(lg, axis=-1, keepdims=True))
        h_ref[rows, d:] = ex / jnp.sum(ex, axis=-1, keepdims=True)


def _outproj(pool, attn, x2, w_out, g1, sc2, sh2, ln_g, ln_b, rw_pad, bsz, n_tok):
    tm = OUTPROJ_TM
    tpb = n_tok // tm
    d = x2.shape[1]
    n_rows = bsz * n_tok
    est = d * d * 4 + 2 * tm * d * 4 * 4 + 6 * tm * d * 4 + (8 << 20)
    return pl.pallas_call(
        functools.partial(_outproj_kernel, tiles_per_batch=tpb, sub=SUB_ROWS),
        grid=(bsz * tpb,),
        in_specs=[
            pl.BlockSpec((tm, POOL_WIDTH), lambda i: (i, 0)),
            pl.BlockSpec((tm, ATTN_WIDTH), lambda i: (i, 0)),
            pl.BlockSpec((tm, d), lambda i: (i, 0)),
            pl.BlockSpec((1, d, d), lambda i: (0, 0, 0)),
            pl.BlockSpec((MOD_ROWS, d), lambda i: (0, 0)),
            pl.BlockSpec((MOD_ROWS, d), lambda i: (0, 0)),
            pl.BlockSpec((MOD_ROWS, d), lambda i: (0, 0)),
            pl.BlockSpec((1, d), lambda i: (0, 0)),
            pl.BlockSpec((1, d), lambda i: (0, 0)),
            pl.BlockSpec((d, LANES), lambda i: (0, 0)),
        ],
        out_specs=[
            pl.BlockSpec((tm, d), lambda i: (i, 0)),
            pl.BlockSpec((tm, d + LANES), lambda i: (i, 0)),
        ],
        out_shape=[
            jax.ShapeDtypeStruct((n_rows, d), F32),
            jax.ShapeDtypeStruct((n_rows, d + LANES), F32),
        ],
        compiler_params=_cparams(("arbitrary",), est),
        name="outproj",
    )(pool, attn, x2, w_out, g1, sc2, sh2, ln_g, ln_b, rw_pad)


def _prefix_incl(flags_bf16, out_ref, n_tok):
    blk = PREFIX_BLK
    tri = (lax.broadcasted_iota(jnp.int32, (blk, blk), 0)
           >= lax.broadcasted_iota(jnp.int32, (blk, blk), 1)).astype(BF16)
    carry = jnp.zeros((1, LANES), F32)
    for k in range(n_tok // blk):
        p = jnp.dot(tri, flags_bf16[k * blk:(k + 1) * blk], preferred_element_type=F32) + carry
        out_ref[k * blk:(k + 1) * blk, :] = p
        carry = p[blk - 1:blk, :]


def _topk_kernel(aff_ref, idx_ref, cnt_ref, part_ref, *, n_tok, cap):
    lane = lax.broadcasted_iota(jnp.int32, (1, LANES), 1)
    bits = jnp.where(lane < N_EXPERTS, pltpu.bitcast(aff_ref[...], jnp.int32), -1)

    seg = n_tok // COUNT_CHAINS

    def search(_, carry):
        lo, hi = carry
        mid = lo + ((hi - lo + 1) >> 1)
        n_ge = sum(jnp.sum((bits[k * seg:(k + 1) * seg] >= mid).astype(jnp.int32), axis=0, keepdims=True)
                   for k in range(COUNT_CHAINS))
        ok = n_ge >= cap
        return jnp.where(ok, mid, lo), jnp.where(ok, hi, mid - 1)

    lo0 = jnp.zeros((1, LANES), jnp.int32)
    hi0 = jnp.full((1, LANES), 0x7F800000, jnp.int32)
    thr, _ = lax.fori_loop(0, 32, search, (lo0, hi0))

    gt = bits > thr
    eq = bits == thr
    need = (cap - jnp.sum(gt.astype(jnp.int32), axis=0, keepdims=True)).astype(F32)
    _prefix_incl(jnp.where(eq, 1.0, 0.0).astype(BF16), cnt_ref, n_tok)
    sel = gt | (eq & (cnt_ref[...] <= need))
    _prefix_incl(jnp.where(sel, 1.0, 0.0).astype(BF16), cnt_ref, n_tok)

    slot = lax.broadcasted_iota(jnp.int32, (1, cap), 1).astype(F32)
    blk = PREFIX_BLK
    part_ref[...] = jnp.zeros_like(part_ref)

    def count(k, _):
        r0 = pl.multiple_of(k * blk, blk)
        cblk = cnt_ref[pl.ds(r0, blk), :]
        for e in range(N_EXPERTS):
            hit = (cblk[:, e:e + 1] <= slot).astype(F32)
            part_ref[e] += jnp.sum(hit.reshape(blk // SUBLANES, SUBLANES, cap), axis=0)
        return 0

    lax.fori_loop(0, n_tok // blk, count, 0)
    for e in range(N_EXPERTS):
        idx_ref[0, e:e + 1, :] = jnp.sum(part_ref[e], axis=0, keepdims=True).astype(jnp.int32)


def _topk(h2ext, bsz, n_tok, cap):
    aff_blk = h2ext.shape[1] // LANES - 1
    return pl.pallas_call(
        functools.partial(_topk_kernel, n_tok=n_tok, cap=cap),
        grid=(bsz,),
        in_specs=[pl.BlockSpec((n_tok, LANES), lambda b: (b, aff_blk))],
        out_specs=pl.BlockSpec((1, N_EXPERTS, cap), lambda b: (b, 0, 0)),
        out_shape=jax.ShapeDtypeStruct((bsz, N_EXPERTS, cap), jnp.int32),
        scratch_shapes=[pltpu.VMEM((n_tok, LANES), F32), pltpu.VMEM((N_EXPERTS, SUBLANES, cap), F32)],
        compiler_params=_cparams(("arbitrary",), 32 << 20),
        name="topk",
    )(h2ext)


def _moe_kernel(idxp_ref, idxc_ref, idxn_ref, h_hbm, acc_in_hbm, wg_ref, wu_ref, wd_ref, g2_ref, acc_hbm,
                xe_ref, xb_ref, gate_ref, y_ref, ab_ref, sem, *, n_f, n_rows, d, cap, n_x, n_sc):
    del acc_in_hbm
    e = pl.program_id(0)
    f = pl.program_id(1)
    n_e = pl.num_programs(0)
    sem_x, sem_a, sem_s = sem.at[0], sem.at[1], sem.at[2]
    rx = n_rows // n_x
    rs = n_rows // n_sc

    def x_copy(idx_ref, j, g0, k):
        return pltpu.make_async_copy(h_hbm.at[pl.ds(idx_ref[0, 0, j], 1), :],
                                     xe_ref.at[pl.ds(g0, SUBLANES), :].at[pl.ds(k, 1), :], sem_x)

    def a_copy(idx_ref, j, g0, k):
        return pltpu.make_async_copy(acc_hbm.at[pl.ds(idx_ref[0, 0, j], 1), :],
                                     ab_ref.at[pl.ds(g0, SUBLANES), :].at[pl.ds(k, 1), :], sem_a)

    def s_copy(idx_ref, j, g0, k):
        return pltpu.make_async_copy(ab_ref.at[pl.ds(g0, SUBLANES), :].at[pl.ds(k, 1), :],
                                     acc_hbm.at[pl.ds(idx_ref[0, 0, j], 1), :], sem_s)

    def start_rows(make, idx_ref, j0, n):
        def body(g, _):
            g0 = pl.multiple_of(j0 + g * SUBLANES, SUBLANES)
            for k in range(SUBLANES):
                make(idx_ref, g0 + k, g0, k).start()
            return 0

        lax.fori_loop(0, n // SUBLANES, body, 0)

    @pl.when((e == 0) & (f == 0))
    def _():
        start_rows(x_copy, idxc_ref, 0, n_rows)

    @pl.when(f == 0)
    def _():
        pltpu.make_async_copy(h_hbm.at[pl.ds(0, n_rows), :], xe_ref, sem_x).wait()
        xb_ref[...] = xe_ref[:, :d].astype(BF16)
        lane = lax.broadcasted_iota(jnp.int32, (n_rows, LANES), 1)
        gate_ref[...] = jnp.sum(jnp.where(lane == e, xe_ref[:, d:], 0.0), axis=-1, keepdims=True)
        y_ref[...] = jnp.zeros_like(y_ref)

    @pl.when((f >= 1) & (f <= n_x) & (e + 1 < n_e))
    def _():
        start_rows(x_copy, idxn_ref, (f - 1) * rx, rx)

    @pl.when((f >= 1) & (f <= n_sc) & (e > 0))
    def _():
        start_rows(s_copy, idxp_ref, (f - 1) * rs, rs)

    @pl.when((f == n_sc + 1) & (e > 0))
    def _():
        pltpu.make_async_copy(ab_ref, acc_hbm.at[pl.ds(0, n_rows), :], sem_s).wait()

    @pl.when((f >= n_sc + 1) & (f <= 2 * n_sc))
    def _():
        start_rows(a_copy, idxc_ref, (f - n_sc - 1) * rs, rs)

    x = xb_ref[...]
    a = jnp.dot(x, wg_ref[0, 0].astype(BF16), preferred_element_type=F32)
    b = jnp.dot(x, wu_ref[0, 0].astype(BF16), preferred_element_type=F32)
    hm = (a * _sigmoid(a) * b).astype(BF16)
    y_ref[...] += jnp.dot(hm, wd_ref[0, 0].astype(BF16), preferred_element_type=F32)

    @pl.when(f == n_f - 1)
    def _():
        pltpu.make_async_copy(acc_hbm.at[pl.ds(0, n_rows), :], ab_ref, sem_a).wait()
        for bi in range(n_rows // cap):
            rows = slice(bi * cap, (bi + 1) * cap)
            ab_ref[rows, :] += (gate_ref[rows, :] * g2_ref[bi:bi + 1, :]) * y_ref[rows, :]

        @pl.when(e == n_e - 1)
        def _():
            start_rows(s_copy, idxc_ref, 0, n_rows)
            pltpu.make_async_copy(ab_ref, acc_hbm.at[pl.ds(0, n_rows), :], sem_s).wait()


def _moe(idx_rows, h2ext, acc0, w_gate, w_up, w_down, g2, cap):
    d = acc0.shape[1]
    n_rows = idx_rows.shape[2]
    ff = w_gate.shape[3]
    tf = MOE_TF
    n_f = ff // tf
    n_x, n_sc = MOE_XGATHER_STEPS, MOE_SCATTER_STEPS
    assert n_x + 1 <= n_f and 2 * n_sc + 1 <= n_f - 1 and n_rows % n_x == 0 and n_rows % n_sc == 0
    est = (2 * 3 * d * tf * 4 + n_rows * h2ext.shape[1] * 4 + n_rows * d * 2 + n_rows * LANES * 4
           + 2 * n_rows * d * 4 + 3 * d * tf * 2 + 4 * n_rows * tf * 4 + (6 << 20))
    last = N_EXPERTS - 1
    return pl.pallas_call(
        functools.partial(_moe_kernel, n_f=n_f, n_rows=n_rows, d=d, cap=cap, n_x=n_x, n_sc=n_sc),
        grid=(N_EXPERTS, n_f),
        in_specs=[
            pl.BlockSpec((1, 1, n_rows), lambda e, f: (jnp.maximum(e - 1, 0), 0, 0), memory_space=pltpu.SMEM),
            pl.BlockSpec((1, 1, n_rows), lambda e, f: (e, 0, 0), memory_space=pltpu.SMEM),
            pl.BlockSpec((1, 1, n_rows), lambda e, f: (jnp.minimum(e + 1, last), 0, 0), memory_space=pltpu.SMEM),
            pl.BlockSpec(memory_space=pl.ANY),
            pl.BlockSpec(memory_space=pl.ANY),
            pl.BlockSpec((1, 1, d, tf), lambda e, f: (0, e, 0, f)),
            pl.BlockSpec((1, 1, d, tf), lambda e, f: (0, e, 0, f)),
            pl.BlockSpec((1, 1, tf, d), lambda e, f: (0, e, f, 0)),
            pl.BlockSpec((MOD_ROWS, d), lambda e, f: (0, 0)),
        ],
        out_specs=pl.BlockSpec(memory_space=pl.ANY),
        out_shape=jax.ShapeDtypeStruct(acc0.shape, F32),
        scratch_shapes=[
            pltpu.VMEM((n_rows, h2ext.shape[1]), F32),
            pltpu.VMEM((n_rows, d), BF16),
            pltpu.VMEM((n_rows, 1), F32),
            pltpu.VMEM((n_rows, d), F32),
            pltpu.VMEM((n_rows, d), F32),
            pltpu.SemaphoreType.DMA((3,)),
        ],
        input_output_aliases={4: 0},
        compiler_params=_cparams(("arbitrary", "arbitrary"), est),
        name="moe",
    )(idx_rows, idx_rows, idx_rows, h2ext, acc0, w_gate, w_up, w_down, g2)


def _final_kernel(a_ref, g_ref, b_ref, o_ref):
    o_ref[...] = _layer_norm(a_ref[...], g_ref[...], b_ref[...])


def _final(acc, ln_g, ln_b):
    n_rows, d = acc.shape
    tm = FINAL_TM
    return pl.pallas_call(
        _final_kernel,
        grid=(n_rows // tm,),
        in_specs=[
            pl.BlockSpec((tm, d), lambda i: (i, 0)),
            pl.BlockSpec((1, d), lambda i: (0, 0)),
            pl.BlockSpec((1, d), lambda i: (0, 0)),
        ],
        out_specs=pl.BlockSpec((tm, d), lambda i: (i, 0)),
        out_shape=jax.ShapeDtypeStruct((n_rows, d), F32),
        compiler_params=_cparams(("arbitrary",), 8 * tm * d * 4 + (8 << 20)),
        name="final_ln",
    )(acc, ln_g, ln_b)


def _rope_tables(n_tok):
    rows = n_tok // GRID_W
    row = jnp.broadcast_to(jnp.arange(rows, dtype=F32)[:, None], (rows, GRID_W)).reshape(-1)
    col = jnp.broadcast_to(jnp.arange(GRID_W, dtype=F32)[None, :], (rows, GRID_W)).reshape(-1)
    inv_freq = ROPE_THETA ** (-jnp.arange(0, ROPE_AXIS_DIM, 2, dtype=F32) / ROPE_AXIS_DIM)
    ang_r = row[:, None] * inv_freq[None, :]
    ang_c = col[:, None] * inv_freq[None, :]
    ang = jnp.concatenate([ang_r, ang_r, ang_c, ang_c], axis=-1)
    cos, sin = jnp.cos(ang), jnp.sin(ang)
    low = (np.arange(HEAD_DIM) % ROPE_AXIS_DIM) < ROPE_AXIS_DIM // 2
    sin_lo = jnp.where(low[None, :], -sin, 0.0)
    sin_hi = jnp.where(low[None, :], 0.0, sin)
    return cos, sin_lo, sin_hi


def kernel(x, c, ctx, c_ctx, ada_w, ada_b, w_in, q_norm, k_norm, pool_w, pool_scale, w_out,
           ln1_g, ln1_b, router_w, w_gate, w_up, w_down, ln2_g, ln2_b):
    bsz, n_tok, d = x.shape
    n_ctx = ctx.shape[1]
    assert ada_w.shape[0] == DEPTH and d == D_MODEL and bsz + 1 <= MOD_ROWS
    cap = CAPACITY_FACTOR * n_tok // N_EXPERTS

    c8 = jnp.concatenate([c, c_ctx[None, :], jnp.zeros((MOD_ROWS - bsz - 1, d), F32)], axis=0)
    mod = _ada(c8, ada_w, ada_b)
    sh1, sc1, g1, sh2, sc2, g2 = [mod[:, k * d:(k + 1) * d] for k in range(6)]

    x2 = x.reshape(bsz * n_tok, d)
    cos, sin_lo, sin_hi = _rope_tables(n_tok)
    u, q, kt, v = _inproj(x2, sc1, sh1, w_in, q_norm, k_norm, cos, sin_lo, sin_hi, bsz, n_tok)
    kct, vc = _ctxkv(ctx.reshape(bsz * n_ctx, d), sc1, sh1, w_in, k_norm, bsz, n_ctx)

    pool = _pool(u, pool_w, pool_scale, bsz, n_tok)
    attn = _attn(q, kt, v, kct, vc, bsz, n_tok, n_ctx)

    rw_pad = jnp.pad(router_w[0], ((0, 0), (0, LANES - N_EXPERTS)))
    acc0, h2ext = _outproj(pool, attn, x2, w_out, g1, sc2, sh2, ln1_g, ln1_b, rw_pad, bsz, n_tok)

    idx = _topk(h2ext, bsz, n_tok, cap)
    idx_rows = (idx + (jnp.arange(bsz, dtype=jnp.int32) * n_tok)[:, None, None])
    idx_rows = idx_rows.transpose(1, 0, 2).reshape(N_EXPERTS, 1, bsz * cap)

    acc = _moe(idx_rows, h2ext, acc0, w_gate, w_up, w_down, g2, cap)
    out = _final(acc, ln2_g, ln2_b)
    return out.reshape(bsz, n_tok, d)
```

```python
import functools

import jax
import jax.numpy as jnp
import numpy as np
from jax import lax
from jax.experimental import pallas as pl
from jax.experimental.pallas import tpu as pltpu

F32 = jnp.float32
BF16 = jnp.bfloat16

D_MODEL = 2048
GRID_W = 64
HEAD_DIM = 128
ATTN_WIDTH = D_MODEL // 2
N_HEADS = ATTN_WIDTH // HEAD_DIM
N_KV_HEADS = 2
Q_PER_KV = N_HEADS // N_KV_HEADS
KV_WIDTH = N_KV_HEADS * HEAD_DIM
POOL_WIDTH = D_MODEL - ATTN_WIDTH
POOL_WINDOWS = (2, 4, 8, 16)
POOL_GROUP = POOL_WIDTH // len(POOL_WINDOWS)
Q_OFF = POOL_WIDTH
K_OFF = Q_OFF + ATTN_WIDTH
V_OFF = K_OFF + KV_WIDTH
IN_WIDTH = V_OFF + KV_WIDTH
ROPE_AXIS_DIM = HEAD_DIM // 2
ROPE_THETA = 10000.0
ATTN_SCALE = HEAD_DIM ** -0.5
N_EXPERTS = 16
CAPACITY_FACTOR = 2
LN_EPS = 1e-6
QK_EPS = 1e-6
DEPTH = 1
ALPHA = (2.0 * DEPTH) ** 0.25

LANES = 128
SUBLANES = 8
V7X_VMEM_BYTES = 64 * 1024 * 1024
MOD_ROWS = SUBLANES
HALO = SUBLANES

ADA_TN = 1024
SUB_ROWS = 256
INPROJ_TM = 512
POOL_TM = 512
ATTN_TQ = 256
ATTN_TK = 512
OUTPROJ_TM = 512
MOE_TF = 256
MOE_XGATHER_STEPS = 16
MOE_SCATTER_STEPS = 8
FINAL_TM = 512
PREFIX_BLK = 256
COUNT_CHAINS = 8


def _vmem_limit(nbytes):
    return int(min(nbytes, V7X_VMEM_BYTES - 4 * 1024 * 1024))


def _cparams(sem, nbytes):
    return pltpu.CompilerParams(dimension_semantics=sem, vmem_limit_bytes=_vmem_limit(nbytes))


def _sigmoid(x):
    return 1.0 / (1.0 + jnp.exp(-x))


def _ada_kernel(c_ref, w_ref, b_ref, o_ref):
    cv = c_ref[...]
    s = cv * _sigmoid(cv)
    o_ref[...] = jnp.dot(s, w_ref[0], preferred_element_type=F32) + b_ref[...]


def _ada(c8, ada_w, ada_b):
    d = c8.shape[1]
    n_out = ada_w.shape[2]
    return pl.pallas_call(
        _ada_kernel,
        grid=(n_out // ADA_TN,),
        in_specs=[
            pl.BlockSpec((MOD_ROWS, d), lambda j: (0, 0)),
            pl.BlockSpec((1, d, ADA_TN), lambda j: (0, 0, j)),
            pl.BlockSpec((1, ADA_TN), lambda j: (0, j)),
        ],
        out_specs=pl.BlockSpec((MOD_ROWS, ADA_TN), lambda j: (0, j)),
        out_shape=jax.ShapeDtypeStruct((MOD_ROWS, n_out), F32),
        compiler_params=_cparams(("arbitrary",), 2 * d * ADA_TN * 4 + (8 << 20)),
        name="ada",
    )(c8, ada_w, ada_b)


def _rms(xh, g):
    ms = jnp.mean(xh * xh, axis=-1, keepdims=True)
    return xh * lax.rsqrt(ms + QK_EPS) * g


def _rope(xn, cos, sin_lo, sin_hi):
    return (xn * cos + pltpu.roll(xn, HEAD_DIM - ROPE_AXIS_DIM // 2, 1) * sin_lo
            + pltpu.roll(xn, ROPE_AXIS_DIM // 2, 1) * sin_hi)


def _inproj_kernel(x_ref, sc_ref, sh_ref, w_ref, qn_ref, kn_ref, cos_ref, slo_ref, shi_ref,
                   u_ref, q_ref, kt_ref, v_ref, *, tiles_per_batch, sub):
    b = pl.program_id(0) // tiles_per_batch
    sc = 1.0 + sc_ref[pl.ds(b, 1), :]
    sh = sh_ref[pl.ds(b, 1), :]
    qn = qn_ref[...]
    kn = kn_ref[...]
    for r0 in range(0, x_ref.shape[0], sub):
        rows = slice(r0, r0 + sub)
        h = x_ref[rows, :] * sc + sh
        proj = jnp.dot(h, w_ref[0], preferred_element_type=F32)
        u_ref[rows, :] = proj[:, :Q_OFF]
        cos = cos_ref[rows, :]
        slo = slo_ref[rows, :]
        shi = shi_ref[rows, :]
        for hd in range(N_HEADS):
            xh = proj[:, Q_OFF + hd * HEAD_DIM:Q_OFF + (hd + 1) * HEAD_DIM]
            qr = _rope(_rms(xh, qn), cos, slo, shi) * ATTN_SCALE
            q_ref[0, hd, rows, :] = qr.astype(BF16)
        for hd in range(N_KV_HEADS):
            kh = proj[:, K_OFF + hd * HEAD_DIM:K_OFF + (hd + 1) * HEAD_DIM]
            kr = _rope(_rms(kh, kn), cos, slo, shi)
            kt_ref[0, hd, :, rows] = kr.T.astype(BF16)
            v_ref[0, hd, rows, :] = proj[:, V_OFF + hd * HEAD_DIM:V_OFF + (hd + 1) * HEAD_DIM].astype(BF16)


def _inproj(x2, sc1, sh1, w_in, q_norm, k_norm, cos, sin_lo, sin_hi, bsz, n_tok):
    tm = INPROJ_TM
    tpb = n_tok // tm
    d = x2.shape[1]
    est = (2 * tm * d * 4 + d * IN_WIDTH * 4 + 2 * tm * IN_WIDTH * 4 + 2 * tm * IN_WIDTH * 4
           + (8 << 20))
    return pl.pallas_call(
        functools.partial(_inproj_kernel, tiles_per_batch=tpb, sub=SUB_ROWS),
        grid=(bsz * tpb,),
        in_specs=[
            pl.BlockSpec((tm, d), lambda i: (i, 0)),
            pl.BlockSpec((MOD_ROWS, d), lambda i: (0, 0)),
            pl.BlockSpec((MOD_ROWS, d), lambda i: (0, 0)),
            pl.BlockSpec((1, d, IN_WIDTH), lambda i: (0, 0, 0)),
            pl.BlockSpec((1, HEAD_DIM), lambda i: (0, 0)),
            pl.BlockSpec((1, HEAD_DIM), lambda i: (0, 0)),
            pl.BlockSpec((tm, HEAD_DIM), lambda i: (i % tpb, 0)),
            pl.BlockSpec((tm, HEAD_DIM), lambda i: (i % tpb, 0)),
            pl.BlockSpec((tm, HEAD_DIM), lambda i: (i % tpb, 0)),
        ],
        out_specs=[
            pl.BlockSpec((tm, POOL_WIDTH), lambda i: (i, 0)),
            pl.BlockSpec((1, N_HEADS, tm, HEAD_DIM), lambda i: (i // tpb, 0, i % tpb, 0)),
            pl.BlockSpec((1, N_KV_HEADS, HEAD_DIM, tm), lambda i: (i // tpb, 0, 0, i % tpb)),
            pl.BlockSpec((1, N_KV_HEADS, tm, HEAD_DIM), lambda i: (i // tpb, 0, i % tpb, 0)),
        ],
        out_shape=[
            jax.ShapeDtypeStruct((bsz * n_tok, POOL_WIDTH), F32),
            jax.ShapeDtypeStruct((bsz, N_HEADS, n_tok, HEAD_DIM), BF16),
            jax.ShapeDtypeStruct((bsz, N_KV_HEADS, HEAD_DIM, n_tok), BF16),
            jax.ShapeDtypeStruct((bsz, N_KV_HEADS, n_tok, HEAD_DIM), BF16),
        ],
        compiler_params=_cparams(("arbitrary",), est),
        name="inproj",
    )(x2, sc1, sh1, w_in, q_norm, k_norm, cos, sin_lo, sin_hi)


def _ctxkv_kernel(ctx_ref, sc_ref, sh_ref, w_ref, kn_ref, kt_ref, v_ref, *, bsz, n_ctx):
    hc = ctx_ref[...] * (1.0 + sc_ref[2:3, :]) + sh_ref[2:3, :]
    kv = jnp.dot(hc, w_ref[0], preferred_element_type=F32)
    kn = kn_ref[...]
    for b in range(bsz):
        rows = slice(b * n_ctx, (b + 1) * n_ctx)
        for hd in range(N_KV_HEADS):
            kh = kv[rows, hd * HEAD_DIM:(hd + 1) * HEAD_DIM]
            kt_ref[b, hd] = _rms(kh, kn).T.astype(BF16)
            v_ref[b, hd] = kv[rows, KV_WIDTH + hd * HEAD_DIM:KV_WIDTH + (hd + 1) * HEAD_DIM].astype(BF16)


def _ctxkv(ctx2, sc1, sh1, w_in, k_norm, bsz, n_ctx):
    d = ctx2.shape[1]
    kvw = 2 * KV_WIDTH
    return pl.pallas_call(
        functools.partial(_ctxkv_kernel, bsz=bsz, n_ctx=n_ctx),
        grid=(1,),
        in_specs=[
            pl.BlockSpec((bsz * n_ctx, d), lambda i: (0, 0)),
            pl.BlockSpec((MOD_ROWS, d), lambda i: (0, 0)),
            pl.BlockSpec((MOD_ROWS, d), lambda i: (0, 0)),
            pl.BlockSpec((1, d, kvw), lambda i: (0, 0, K_OFF // kvw)),
            pl.BlockSpec((1, HEAD_DIM), lambda i: (0, 0)),
        ],
        out_specs=[
            pl.BlockSpec((bsz, N_KV_HEADS, HEAD_DIM, n_ctx), lambda i: (0, 0, 0, 0)),
            pl.BlockSpec((bsz, N_KV_HEADS, n_ctx, HEAD_DIM), lambda i: (0, 0, 0, 0)),
        ],
        out_shape=[
            jax.ShapeDtypeStruct((bsz, N_KV_HEADS, HEAD_DIM, n_ctx), BF16),
            jax.ShapeDtypeStruct((bsz, N_KV_HEADS, n_ctx, HEAD_DIM), BF16),
        ],
        compiler_params=_cparams(("arbitrary",), 2 * (bsz * n_ctx * d * 4 + d * kvw * 4) + (8 << 20)),
        name="ctxkv",
    )(ctx2, sc1, sh1, w_in, k_norm)


def _pool_kernel(up_ref, uc_ref, un_ref, w_ref, s_ref, o_ref, ext_ref, *, tiles_per_batch, tm, n_tok):
    it = pl.program_id(0) % tiles_per_batch
    ext_ref[0:HALO, :] = jnp.where(it == 0, 0.0, up_ref[...])
    ext_ref[HALO:HALO + tm, :] = uc_ref[...]
    ext_ref[HALO + tm:HALO + tm + HALO, :] = jnp.where(it == tiles_per_batch - 1, 0.0, un_ref[...])
    t = it * tm + lax.broadcasted_iota(jnp.int32, (tm, POOL_GROUP), 0)
    for gi, w in enumerate(POOL_WINDOWS):
        half = w // 2
        cols = slice(gi * POOL_GROUP, (gi + 1) * POOL_GROUP)
        acc = ext_ref[HALO - half:HALO - half + tm, cols]
        for dd in range(-half + 1, half):
            acc = acc + ext_ref[HALO + dd:HALO + dd + tm, cols]
        cnt = (jnp.minimum(t + half, n_tok) - jnp.maximum(t - half, 0)).astype(F32)
        diff = acc / cnt - uc_ref[:, cols]
        mixed = jnp.dot(diff, w_ref[0, gi], preferred_element_type=F32) * s_ref[:, cols]
        o_ref[:, cols] = mixed.astype(BF16)


def _pool(u, pool_w, pool_scale, bsz, n_tok):
    tm = POOL_TM
    tpb = n_tok // tm
    n_rows = bsz * n_tok
    hb = tm // HALO
    last_hblk = n_rows // HALO - 1
    return pl.pallas_call(
        functools.partial(_pool_kernel, tiles_per_batch=tpb, tm=tm, n_tok=n_tok),
        grid=(bsz * tpb,),
        in_specs=[
            pl.BlockSpec((HALO, POOL_WIDTH), lambda i: (jnp.maximum(i * hb - 1, 0), 0)),
            pl.BlockSpec((tm, POOL_WIDTH), lambda i: (i, 0)),
            pl.BlockSpec((HALO, POOL_WIDTH), lambda i: (jnp.minimum((i + 1) * hb, last_hblk), 0)),
            pl.BlockSpec((1, len(POOL_WINDOWS), POOL_GROUP, POOL_GROUP), lambda i: (0, 0, 0, 0)),
            pl.BlockSpec((1, POOL_WIDTH), lambda i: (0, 0)),
        ],
        out_specs=pl.BlockSpec((tm, POOL_WIDTH), lambda i: (i, 0)),
        out_shape=jax.ShapeDtypeStruct((n_rows, POOL_WIDTH), BF16),
        scratch_shapes=[pltpu.VMEM((tm + 2 * HALO, POOL_WIDTH), F32)],
        compiler_params=_cparams(("arbitrary",), 6 * tm * POOL_WIDTH * 4 + (8 << 20)),
        name="pool",
    )(u, u, u, pool_w, pool_scale)


def _attn_kernel(q_ref, kt_ref, v_ref, kct_ref, vc_ref, o_ref, *, tq, tk, n_tok):
    rows = Q_PER_KV * tq
    q = q_ref[0].reshape(rows, HEAD_DIM)

    def step(carry, kt_c, v_c):
        m, l, acc = carry
        s = jnp.dot(q, kt_c, preferred_element_type=F32)
        m_new = jnp.maximum(m, jnp.max(s, axis=-1, keepdims=True))
        alpha = jnp.exp(m - m_new)
        p = jnp.exp(s - m_new)
        l = alpha * l + jnp.sum(p, axis=-1, keepdims=True)
        acc = alpha * acc + jnp.dot(p.astype(BF16), v_c, preferred_element_type=F32)
        return m_new, l, acc

    carry = (jnp.full((rows, 1), -jnp.inf, F32), jnp.zeros((rows, 1), F32),
             jnp.zeros((rows, HEAD_DIM), F32))
    carry = step(carry, kct_ref[0, 0], vc_ref[0, 0])
    for j in range(n_tok // tk):
        carry = step(carry, kt_ref[0, 0, :, j * tk:(j + 1) * tk], v_ref[0, 0, j * tk:(j + 1) * tk, :])
    _, l, acc = carry
    out = acc / l
    for g in range(Q_PER_KV):
        o_ref[:, g * HEAD_DIM:(g + 1) * HEAD_DIM] = out[g * tq:(g + 1) * tq].astype(BF16)


def _attn(q, kt, v, kct, vc, bsz, n_tok, n_ctx):
    tq, tk = ATTN_TQ, ATTN_TK
    nq = n_tok // tq
    gw = Q_PER_KV * HEAD_DIM
    return pl.pallas_call(
        functools.partial(_attn_kernel, tq=tq, tk=tk, n_tok=n_tok),
        grid=(bsz, N_KV_HEADS, nq),
        in_specs=[
            pl.BlockSpec((1, Q_PER_KV, tq, HEAD_DIM), lambda b, h, i: (b, h, i, 0)),
            pl.BlockSpec((1, 1, HEAD_DIM, n_tok), lambda b, h, i: (b, h, 0, 0)),
            pl.BlockSpec((1, 1, n_tok, HEAD_DIM), lambda b, h, i: (b, h, 0, 0)),
            pl.BlockSpec((1, 1, HEAD_DIM, n_ctx), lambda b, h, i: (b, h, 0, 0)),
            pl.BlockSpec((1, 1, n_ctx, HEAD_DIM), lambda b, h, i: (b, h, 0, 0)),
        ],
        out_specs=pl.BlockSpec((tq, gw), lambda b, h, i: (b * nq + i, h)),
        out_shape=jax.ShapeDtypeStruct((bsz * n_tok, ATTN_WIDTH), BF16),
        compiler_params=_cparams(("arbitrary", "arbitrary", "arbitrary"), 40 << 20),
        name="attn",
    )(q, kt, v, kct, vc)


def _layer_norm(r, g, b):
    mu = jnp.mean(r, axis=-1, keepdims=True)
    rc = r - mu
    var = jnp.mean(rc * rc, axis=-1, keepdims=True)
    return rc * lax.rsqrt(var + LN_EPS) * g + b


def _split_bf16(x):
    hi = x.astype(BF16)
    lo = (x - hi.astype(F32)).astype(BF16)
    return hi, lo


def _outproj_kernel(pool_ref, attn_ref, x_ref, w_ref, g1_ref, sc_ref, sh_ref, lg_ref, lb_ref, rw_ref,
                    acc_ref, h_ref, *, tiles_per_batch, sub):
    b = pl.program_id(0) // tiles_per_batch
    g1 = g1_ref[pl.ds(b, 1), :]
    sc = 1.0 + sc_ref[pl.ds(b, 1), :]
    sh = sh_ref[pl.ds(b, 1), :]
    rh, rl = _split_bf16(rw_ref[...])
    d = x_ref.shape[1]
    for r0 in range(0, x_ref.shape[0], sub):
        rows = slice(r0, r0 + sub)
        y = (jnp.dot(pool_ref[rows, :].astype(F32), w_ref[0, :POOL_WIDTH, :], preferred_element_type=F32)
             + jnp.dot(attn_ref[rows, :].astype(F32), w_ref[0, POOL_WIDTH:, :], preferred_element_type=F32))
        r = ALPHA * x_ref[rows, :] + g1 * y
        xm = _layer_norm(r, lg_ref[...], lb_ref[...])
        acc_ref[rows, :] = ALPHA * xm
        h2 = xm * sc + sh
        h_ref[rows, :d] = h2
        hh, hl = _split_bf16(h2)
        logits = (jnp.dot(hh, rh, preferred_element_type=F32) + jnp.dot(hl, rh, preferred_element_type=F32)
                  + jnp.dot(hh, rl, preferred_element_type=F32))
        valid = lax.broadcasted_iota(jnp.int32, logits.shape, 1) < N_EXPERTS
        lg = jnp.where(valid, logits, -jnp.inf)
        ex = jnp.exp(lg - jnp.max(lg, axis=-1, keepdims=True))
        h_ref[rows, d:] = ex / jnp.sum(ex, axis=-1, keepdims=True)


def _outproj(pool, attn, x2, w_out, g1, sc2, sh2, ln_g, ln_b, rw_pad, bsz, n_tok):
    tm = OUTPROJ_TM
    tpb = n_tok // tm
    d = x2.shape[1]
    n_rows = bsz * n_tok
    est = d * d * 4 + 2 * tm * d * 4 * 4 + 6 * tm * d * 4 + (8 << 20)
    return pl.pallas_call(
        functools.partial(_outproj_kernel, tiles_per_batch=tpb, sub=SUB_ROWS),
        grid=(bsz * tpb,),
        in_specs=[
            pl.BlockSpec((tm, POOL_WIDTH), lambda i: (i, 0)),
            pl.BlockSpec((tm, ATTN_WIDTH), lambda i: (i, 0)),
            pl.BlockSpec((tm, d), lambda i: (i, 0)),
            pl.BlockSpec((1, d, d), lambda i: (0, 0, 0)),
            pl.BlockSpec((MOD_ROWS, d), lambda i: (0, 0)),
            pl.BlockSpec((MOD_ROWS, d), lambda i: (0, 0)),
            pl.BlockSpec((MOD_ROWS, d), lambda i: (0, 0)),
            pl.BlockSpec((1, d), lambda i: (0, 0)),
            pl.BlockSpec((1, d), lambda i: (0, 0)),
            pl.BlockSpec((d, LANES), lambda i: (0, 0)),
        ],
        out_specs=[
            pl.BlockSpec((tm, d), lambda i: (i, 0)),
            pl.BlockSpec((tm, d + LANES), lambda i: (i, 0)),
        ],
        out_shape=[
            jax.ShapeDtypeStruct((n_rows, d), F32),
            jax.ShapeDtypeStruct((n_rows, d + LANES), F32),
        ],
        compiler_params=_cparams(("arbitrary",), est),
        name="outproj",
    )(pool, attn, x2, w_out, g1, sc2, sh2, ln_g, ln_b, rw_pad)


def _prefix_incl(flags_bf16, out_ref, n_tok):
    blk = PREFIX_BLK
    tri = (lax.broadcasted_iota(jnp.int32, (blk, blk), 0)
           >= lax.broadcasted_iota(jnp.int32, (blk, blk), 1)).astype(BF16)
    carry = jnp.zeros((1, LANES), F32)
    for k in range(n_tok // blk):
        p = jnp.dot(tri, flags_bf16[k * blk:(k + 1) * blk], preferred_element_type=F32) + carry
        out_ref[k * blk:(k + 1) * blk, :] = p
        carry = p[blk - 1:blk, :]


def _topk_kernel(aff_ref, idx_ref, cnt_ref, part_ref, *, n_tok, cap):
    lane = lax.broadcasted_iota(jnp.int32, (1, LANES), 1)
    bits = jnp.where(lane < N_EXPERTS, pltpu.bitcast(aff_ref[...], jnp.int32), -1)

    seg = n_tok // COUNT_CHAINS

    def search(_, carry):
        lo, hi = carry
        mid = lo + ((hi - lo + 1) >> 1)
        n_ge = sum(jnp.sum((bits[k * seg:(k + 1) * seg] >= mid).astype(jnp.int32), axis=0, keepdims=True)
                   for k in range(COUNT_CHAINS))
        ok = n_ge >= cap
        return jnp.where(ok, mid, lo), jnp.where(ok, hi, mid - 1)

    lo0 = jnp.zeros((1, LANES), jnp.int32)
    hi0 = jnp.full((1, LANES), 0x7F800000, jnp.int32)
    thr, _ = lax.fori_loop(0, 32, search, (lo0, hi0))

    gt = bits > thr
    eq = bits == thr
    need = (cap - jnp.sum(gt.astype(jnp.int32), axis=0, keepdims=True)).astype(F32)
    _prefix_incl(jnp.where(eq, 1.0, 0.0).astype(BF16), cnt_ref, n_tok)
    sel = gt | (eq & (cnt_ref[...] <= need))
    _prefix_incl(jnp.where(sel, 1.0, 0.0).astype(BF16), cnt_ref, n_tok)

    slot = lax.broadcasted_iota(jnp.int32, (1, cap), 1).astype(F32)
    blk = PREFIX_BLK
    part_ref[...] = jnp.zeros_like(part_ref)

    def count(k, _):
        r0 = pl.multiple_of(k * blk, blk)
        cblk = cnt_ref[pl.ds(r0, blk), :]
        for e in range(N_EXPERTS):
            hit = (cblk[:, e:e + 1] <= slot).astype(F32)
            part_ref[e] += jnp.sum(hit.reshape(blk // SUBLANES, SUBLANES, cap), axis=0)
        return 0

    lax.fori_loop(0, n_tok // blk, count, 0)
    for e in range(N_EXPERTS):
        idx_ref[0, e:e + 1, :] = jnp.sum(part_ref[e], axis=0, keepdims=True).astype(jnp.int32)


def _topk(h2ext, bsz, n_tok, cap):
    aff_blk = h2ext.shape[1] // LANES - 1
    return pl.pallas_call(
        functools.partial(_topk_kernel, n_tok=n_tok, cap=cap),
        grid=(bsz,),
        in_specs=[pl.BlockSpec((n_tok, LANES), lambda b: (b, aff_blk))],
        out_specs=pl.BlockSpec((1, N_EXPERTS, cap), lambda b: (b, 0, 0)),
        out_shape=jax.ShapeDtypeStruct((bsz, N_EXPERTS, cap), jnp.int32),
        scratch_shapes=[pltpu.VMEM((n_tok, LANES), F32), pltpu.VMEM((N_EXPERTS, SUBLANES, cap), F32)],
        compiler_params=_cparams(("arbitrary",), 32 << 20),
        name="topk",
    )(h2ext)


def _moe_kernel(idxp_ref, idxc_ref, idxn_ref, h_hbm, acc_in_hbm, wg_ref, wu_ref, wd_ref, g2_ref, acc_hbm,
                xe_ref, xb_ref, gate_ref, y_ref, ab_ref, sem, *, n_f, n_rows, d, cap, n_x, n_sc):
    del acc_in_hbm
    e = pl.program_id(0)
    f = pl.program_id(1)
    n_e = pl.num_programs(0)
    sem_x, sem_a, sem_s = sem.at[0], sem.at[1], sem.at[2]
    rx = n_rows // n_x
    rs = n_rows // n_sc

    def x_copy(idx_ref, g0, n, k):
        return pltpu.make_async_copy(h_hbm.at[pl.ds(idx_ref[0, 0, g0 + k], 1), :],
                                     xe_ref.at[pl.ds(g0, n), :].at[pl.ds(k, 1), :], sem_x)

    def a_copy(idx_ref, g0, n, k):
        return pltpu.make_async_copy(acc_hbm.at[pl.ds(idx_ref[0, 0, g0 + k], 1), :],
                                     ab_ref.at[pl.ds(g0, n), :].at[pl.ds(k, 1), :], sem_a)

    def s_copy(idx_ref, g0, n, k):
        return pltpu.make_async_copy(ab_ref.at[pl.ds(g0, n), :].at[pl.ds(k, 1), :],
                                     acc_hbm.at[pl.ds(idx_ref[0, 0, g0 + k], 1), :], sem_s)

    def start_window(make, idx_ref, j0, n):
        g0 = pl.multiple_of(j0, SUBLANES)
        for k in range(n):
            make(idx_ref, g0, n, k).start()

    def start_all(make, idx_ref):
        def body(g, _):
            start_window(make, idx_ref, g * SUBLANES, SUBLANES)
            return 0

        lax.fori_loop(0, n_rows // SUBLANES, body, 0)

    def wait_x():
        pltpu.make_async_copy(h_hbm.at[pl.ds(0, n_rows), :], xe_ref, sem_x).wait()

    def wait_a():
        pltpu.make_async_copy(acc_hbm.at[pl.ds(0, n_rows), :], ab_ref, sem_a).wait()

    def wait_s():
        pltpu.make_async_copy(ab_ref, acc_hbm.at[pl.ds(0, n_rows), :], sem_s).wait()

    @pl.when((e == 0) & (f == 0))
    def _():
        start_all(x_copy, idxc_ref)
        start_all(a_copy, idxc_ref)
        wait_a()

    @pl.when(f == 0)
    def _():
        wait_x()
        xb_ref[...] = xe_ref[:, :d].astype(BF16)
        lane = lax.broadcasted_iota(jnp.int32, (n_rows, LANES), 1)
        gate_ref[...] = jnp.sum(jnp.where(lane == e, xe_ref[:, d:], 0.0), axis=-1, keepdims=True)
        y_ref[...] = jnp.zeros_like(y_ref)

    @pl.when(f == n_sc + 1)
    def _():
        wait_s()

    def compute():
        x = xb_ref[...]
        a = jnp.dot(x, wg_ref[0, 0].astype(BF16), preferred_element_type=F32)
        b = jnp.dot(x, wu_ref[0, 0].astype(BF16), preferred_element_type=F32)
        hm = (a * _sigmoid(a) * b).astype(BF16)
        y_ref[...] += jnp.dot(hm, wd_ref[0, 0].astype(BF16), preferred_element_type=F32)

    def step_plain():
        compute()

    def step_scatter():
        start_window(x_copy, idxn_ref, (f - 1) * rx, rx)
        start_window(s_copy, idxp_ref, (f - 1) * rs, rs)
        compute()

    def step_gather():
        start_window(x_copy, idxn_ref, (f - 1) * rx, rx)
        start_window(a_copy, idxc_ref, (f - n_sc - 1) * rs, rs)
        compute()

    phase = jnp.where((f >= 1) & (f <= n_sc), 1, jnp.where((f > n_sc) & (f <= 2 * n_sc), 2, 0))
    lax.switch(phase, [step_plain, step_scatter, step_gather])

    @pl.when(f == n_f - 1)
    def _():
        wait_a()
        for bi in range(n_rows // cap):
            rows = slice(bi * cap, (bi + 1) * cap)
            ab_ref[rows, :] += (gate_ref[rows, :] * g2_ref[bi:bi + 1, :]) * y_ref[rows, :]

        @pl.when(e == n_e - 1)
        def _():
            wait_x()
            start_all(s_copy, idxc_ref)
            wait_s()


def _moe(idx_rows, h2ext, acc0, w_gate, w_up, w_down, g2, cap):
    d = acc0.shape[1]
    n_rows = idx_rows.shape[2]
    ff = w_gate.shape[3]
    tf = MOE_TF
    n_f = ff // tf
    n_x, n_sc = MOE_XGATHER_STEPS, MOE_SCATTER_STEPS
    assert n_x == 2 * n_sc and n_x + 1 <= n_f - 1 and n_rows % (n_x * SUBLANES) == 0
    est = (2 * 3 * d * tf * 4 + n_rows * h2ext.shape[1] * 4 + n_rows * d * 2 + n_rows * LANES * 4
           + 2 * n_rows * d * 4 + 3 * d * tf * 2 + 4 * n_rows * tf * 4 + (6 << 20))
    last = N_EXPERTS - 1
    return pl.pallas_call(
        functools.partial(_moe_kernel, n_f=n_f, n_rows=n_rows, d=d, cap=cap, n_x=n_x, n_sc=n_sc),
        grid=(N_EXPERTS, n_f),
        in_specs=[
            pl.BlockSpec((1, 1, n_rows), lambda e, f: (jnp.maximum(e - 1, 0), 0, 0), memory_space=pltpu.SMEM),
            pl.BlockSpec((1, 1, n_rows), lambda e, f: (e, 0, 0), memory_space=pltpu.SMEM),
            pl.BlockSpec((1, 1, n_rows), lambda e, f: (jnp.minimum(e + 1, last), 0, 0), memory_space=pltpu.SMEM),
            pl.BlockSpec(memory_space=pl.ANY),
            pl.BlockSpec(memory_space=pl.ANY),
            pl.BlockSpec((1, 1, d, tf), lambda e, f: (0, e, 0, f)),
            pl.BlockSpec((1, 1, d, tf), lambda e, f: (0, e, 0, f)),
            pl.BlockSpec((1, 1, tf, d), lambda e, f: (0, e, f, 0)),
            pl.BlockSpec((MOD_ROWS, d), lambda e, f: (0, 0)),
        ],
        out_specs=pl.BlockSpec(memory_space=pl.ANY),
        out_shape=jax.ShapeDtypeStruct(acc0.shape, F32),
        scratch_shapes=[
            pltpu.VMEM((n_rows, h2ext.shape[1]), F32),
            pltpu.VMEM((n_rows, d), BF16),
            pltpu.VMEM((n_rows, 1), F32),
            pltpu.VMEM((n_rows, d), F32),
            pltpu.VMEM((n_rows, d), F32),
            pltpu.SemaphoreType.DMA((3,)),
        ],
        input_output_aliases={4: 0},
        compiler_params=_cparams(("arbitrary", "arbitrary"), est),
        name="moe",
    )(idx_rows, idx_rows, idx_rows, h2ext, acc0, w_gate, w_up, w_down, g2)


def _final_kernel(a_ref, g_ref, b_ref, o_ref):
    o_ref[...] = _layer_norm(a_ref[...], g_ref[...], b_ref[...])


def _final(acc, ln_g, ln_b):
    n_rows, d = acc.shape
    tm = FINAL_TM
    return pl.pallas_call(
        _final_kernel,
        grid=(n_rows // tm,),
        in_specs=[
            pl.BlockSpec((tm, d), lambda i: (i, 0)),
            pl.BlockSpec((1, d), lambda i: (0, 0)),
            pl.BlockSpec((1, d), lambda i: (0, 0)),
        ],
        out_specs=pl.BlockSpec((tm, d), lambda i: (i, 0)),
        out_shape=jax.ShapeDtypeStruct((n_rows, d), F32),
        compiler_params=_cparams(("arbitrary",), 8 * tm * d * 4 + (8 << 20)),
        name="final_ln",
    )(acc, ln_g, ln_b)


def _rope_tables(n_tok):
    rows = n_tok // GRID_W
    row = jnp.broadcast_to(jnp.arange(rows, dtype=F32)[:, None], (rows, GRID_W)).reshape(-1)
    col = jnp.broadcast_to(jnp.arange(GRID_W, dtype=F32)[None, :], (rows, GRID_W)).reshape(-1)
    inv_freq = ROPE_THETA ** (-jnp.arange(0, ROPE_AXIS_DIM, 2, dtype=F32) / ROPE_AXIS_DIM)
    ang_r = row[:, None] * inv_freq[None, :]
    ang_c = col[:, None] * inv_freq[None, :]
    ang = jnp.concatenate([ang_r, ang_r, ang_c, ang_c], axis=-1)
    cos, sin = jnp.cos(ang), jnp.sin(ang)
    low = (np.arange(HEAD_DIM) % ROPE_AXIS_DIM) < ROPE_AXIS_DIM // 2
    sin_lo = jnp.where(low[None, :], -sin, 0.0)
    sin_hi = jnp.where(low[None, :], 0.0, sin)
    return cos, sin_lo, sin_hi


def kernel(x, c, ctx, c_ctx, ada_w, ada_b, w_in, q_norm, k_norm, pool_w, pool_scale, w_out,
           ln1_g, ln1_b, router_w, w_gate, w_up, w_down, ln2_g, ln2_b):
    bsz, n_tok, d = x.shape
    n_ctx = ctx.shape[1]
    assert ada_w.shape[0] == DEPTH and d == D_MODEL and bsz + 1 <= MOD_ROWS
    cap = CAPACITY_FACTOR * n_tok // N_EXPERTS

    c8 = jnp.concatenate([c, c_ctx[None, :], jnp.zeros((MOD_ROWS - bsz - 1, d), F32)], axis=0)
    mod = _ada(c8, ada_w, ada_b)
    sh1, sc1, g1, sh2, sc2, g2 = [mod[:, k * d:(k + 1) * d] for k in range(6)]

    x2 = x.reshape(bsz * n_tok, d)
    cos, sin_lo, sin_hi = _rope_tables(n_tok)
    u, q, kt, v = _inproj(x2, sc1, sh1, w_in, q_norm, k_norm, cos, sin_lo, sin_hi, bsz, n_tok)
    kct, vc = _ctxkv(ctx.reshape(bsz * n_ctx, d), sc1, sh1, w_in, k_norm, bsz, n_ctx)

    pool = _pool(u, pool_w, pool_scale, bsz, n_tok)
    attn = _attn(q, kt, v, kct, vc, bsz, n_tok, n_ctx)

    rw_pad = jnp.pad(router_w[0], ((0, 0), (0, LANES - N_EXPERTS)))
    acc0, h2ext = _outproj(pool, attn, x2, w_out, g1, sc2, sh2, ln1_g, ln1_b, rw_pad, bsz, n_tok)

    idx = _topk(h2ext, bsz, n_tok, cap)
    idx_rows = (idx + (jnp.arange(bsz, dtype=jnp.int32) * n_tok)[:, None, None])
    idx_rows = idx_rows.transpose(1, 0, 2).reshape(N_EXPERTS, 1, bsz * cap)

    acc = _moe(idx_rows, h2ext, acc0, w_gate, w_up, w_down, g2, cap)
    out = _final(acc, ln2_g, ln2_b)
    return out.reshape(bsz, n_tok, d)
```

```python
import functools

import jax
import jax.numpy as jnp
import numpy as np
from jax import lax
from jax.experimental import pallas as pl
from jax.experimental.pallas import tpu as pltpu

F32 = jnp.float32
BF16 = jnp.bfloat16

D_MODEL = 2048
GRID_W = 64
HEAD_DIM = 128
ATTN_WIDTH = D_MODEL // 2
N_HEADS = ATTN_WIDTH // HEAD_DIM
N_KV_HEADS = 2
Q_PER_KV = N_HEADS // N_KV_HEADS
KV_WIDTH = N_KV_HEADS * HEAD_DIM
POOL_WIDTH = D_MODEL - ATTN_WIDTH
POOL_WINDOWS = (2, 4, 8, 16)
POOL_GROUP = POOL_WIDTH // len(POOL_WINDOWS)
Q_OFF = POOL_WIDTH
K_OFF = Q_OFF + ATTN_WIDTH
V_OFF = K_OFF + KV_WIDTH
IN_WIDTH = V_OFF + KV_WIDTH
ROPE_AXIS_DIM = HEAD_DIM // 2
ROPE_THETA = 10000.0
ATTN_SCALE = HEAD_DIM ** -0.5
Q_SCALE = ATTN_SCALE * float(np.log2(np.e))
V_EXT = 2 * HEAD_DIM
N_EXPERTS = 16
CAPACITY_FACTOR = 2
LN_EPS = 1e-6
QK_EPS = 1e-6
DEPTH = 1
ALPHA = (2.0 * DEPTH) ** 0.25

LANES = 128
SUBLANES = 8
V7X_VMEM_BYTES = 64 * 1024 * 1024
MOD_ROWS = SUBLANES
HALO = SUBLANES

ADA_TN = 1024
SUB_ROWS = 256
INPROJ_TM = 512
POOL_TM = 512
ATTN_TQ = 256
ATTN_TK = 512
OUTPROJ_TM = 512
MOE_TF = 512
MOE_XGATHER_STEPS = 8
MOE_SCATTER_STEPS = 4
FINAL_TM = 512
PREFIX_BLK = 256
COUNT_CHAINS = 8


def _vmem_limit(nbytes):
    return int(min(nbytes, V7X_VMEM_BYTES - 4 * 1024 * 1024))


def _cparams(sem, nbytes):
    return pltpu.CompilerParams(dimension_semantics=sem, vmem_limit_bytes=_vmem_limit(nbytes))


def _sigmoid(x):
    return 1.0 / (1.0 + jnp.exp(-x))


def _ada_kernel(c_ref, w_ref, b_ref, o_ref):
    cv = c_ref[...]
    s = cv * _sigmoid(cv)
    o_ref[...] = jnp.dot(s, w_ref[0], preferred_element_type=F32) + b_ref[...]


def _ada(c8, ada_w, ada_b):
    d = c8.shape[1]
    n_out = ada_w.shape[2]
    return pl.pallas_call(
        _ada_kernel,
        grid=(n_out // ADA_TN,),
        in_specs=[
            pl.BlockSpec((MOD_ROWS, d), lambda j: (0, 0)),
            pl.BlockSpec((1, d, ADA_TN), lambda j: (0, 0, j)),
            pl.BlockSpec((1, ADA_TN), lambda j: (0, j)),
        ],
        out_specs=pl.BlockSpec((MOD_ROWS, ADA_TN), lambda j: (0, j)),
        out_shape=jax.ShapeDtypeStruct((MOD_ROWS, n_out), F32),
        compiler_params=_cparams(("arbitrary",), 2 * d * ADA_TN * 4 + (8 << 20)),
        name="ada",
    )(c8, ada_w, ada_b)


def _ones_column(n):
    lane = lax.broadcasted_iota(jnp.int32, (n, V_EXT - HEAD_DIM), 1)
    return jnp.where(lane == 0, 1.0, 0.0).astype(BF16)


def _rms(xh, g):
    ms = jnp.mean(xh * xh, axis=-1, keepdims=True)
    return xh * lax.rsqrt(ms + QK_EPS) * g


def _rope(xn, cos, sin_lo, sin_hi):
    return (xn * cos + pltpu.roll(xn, HEAD_DIM - ROPE_AXIS_DIM // 2, 1) * sin_lo
            + pltpu.roll(xn, ROPE_AXIS_DIM // 2, 1) * sin_hi)


def _inproj_kernel(x_ref, sc_ref, sh_ref, w_ref, qn_ref, kn_ref, cos_ref, slo_ref, shi_ref,
                   u_ref, q_ref, kt_ref, v_ref, *, tiles_per_batch, sub):
    b = pl.program_id(0) // tiles_per_batch
    sc = 1.0 + sc_ref[pl.ds(b, 1), :]
    sh = sh_ref[pl.ds(b, 1), :]
    qn = qn_ref[...]
    kn = kn_ref[...]
    for r0 in range(0, x_ref.shape[0], sub):
        rows = slice(r0, r0 + sub)
        h = x_ref[rows, :] * sc + sh
        proj = jnp.dot(h, w_ref[0], preferred_element_type=F32)
        u_ref[rows, :] = proj[:, :Q_OFF]
        cos = cos_ref[rows, :]
        slo = slo_ref[rows, :]
        shi = shi_ref[rows, :]
        for hd in range(N_HEADS):
            xh = proj[:, Q_OFF + hd * HEAD_DIM:Q_OFF + (hd + 1) * HEAD_DIM]
            qr = _rope(_rms(xh, qn), cos, slo, shi) * Q_SCALE
            q_ref[0, hd, rows, :] = qr.astype(BF16)
        for hd in range(N_KV_HEADS):
            kh = proj[:, K_OFF + hd * HEAD_DIM:K_OFF + (hd + 1) * HEAD_DIM]
            kr = _rope(_rms(kh, kn), cos, slo, shi)
            kt_ref[0, hd, :, rows] = kr.T.astype(BF16)
            v_ref[0, hd, rows, :HEAD_DIM] = proj[:, V_OFF + hd * HEAD_DIM:V_OFF + (hd + 1) * HEAD_DIM].astype(BF16)
            v_ref[0, hd, rows, HEAD_DIM:] = _ones_column(sub)


def _inproj(x2, sc1, sh1, w_in, q_norm, k_norm, cos, sin_lo, sin_hi, bsz, n_tok):
    tm = INPROJ_TM
    tpb = n_tok // tm
    d = x2.shape[1]
    est = (2 * tm * d * 4 + d * IN_WIDTH * 4 + 2 * tm * IN_WIDTH * 4 + 2 * tm * IN_WIDTH * 4
           + (8 << 20))
    return pl.pallas_call(
        functools.partial(_inproj_kernel, tiles_per_batch=tpb, sub=SUB_ROWS),
        grid=(bsz * tpb,),
        in_specs=[
            pl.BlockSpec((tm, d), lambda i: (i, 0)),
            pl.BlockSpec((MOD_ROWS, d), lambda i: (0, 0)),
            pl.BlockSpec((MOD_ROWS, d), lambda i: (0, 0)),
            pl.BlockSpec((1, d, IN_WIDTH), lambda i: (0, 0, 0)),
            pl.BlockSpec((1, HEAD_DIM), lambda i: (0, 0)),
            pl.BlockSpec((1, HEAD_DIM), lambda i: (0, 0)),
            pl.BlockSpec((tm, HEAD_DIM), lambda i: (i % tpb, 0)),
            pl.BlockSpec((tm, HEAD_DIM), lambda i: (i % tpb, 0)),
            pl.BlockSpec((tm, HEAD_DIM), lambda i: (i % tpb, 0)),
        ],
        out_specs=[
            pl.BlockSpec((tm, POOL_WIDTH), lambda i: (i, 0)),
            pl.BlockSpec((1, N_HEADS, tm, HEAD_DIM), lambda i: (i // tpb, 0, i % tpb, 0)),
            pl.BlockSpec((1, N_KV_HEADS, HEAD_DIM, tm), lambda i: (i // tpb, 0, 0, i % tpb)),
            pl.BlockSpec((1, N_KV_HEADS, tm, V_EXT), lambda i: (i // tpb, 0, i % tpb, 0)),
        ],
        out_shape=[
            jax.ShapeDtypeStruct((bsz * n_tok, POOL_WIDTH), F32),
            jax.ShapeDtypeStruct((bsz, N_HEADS, n_tok, HEAD_DIM), BF16),
            jax.ShapeDtypeStruct((bsz, N_KV_HEADS, HEAD_DIM, n_tok), BF16),
            jax.ShapeDtypeStruct((bsz, N_KV_HEADS, n_tok, V_EXT), BF16),
        ],
        compiler_params=_cparams(("arbitrary",), est),
        name="inproj",
    )(x2, sc1, sh1, w_in, q_norm, k_norm, cos, sin_lo, sin_hi)


def _ctxkv_kernel(ctx_ref, sc_ref, sh_ref, w_ref, kn_ref, kt_ref, v_ref, *, bsz, n_ctx):
    hc = ctx_ref[...] * (1.0 + sc_ref[2:3, :]) + sh_ref[2:3, :]
    kv = jnp.dot(hc, w_ref[0], preferred_element_type=F32)
    kn = kn_ref[...]
    for b in range(bsz):
        rows = slice(b * n_ctx, (b + 1) * n_ctx)
        for hd in range(N_KV_HEADS):
            kh = kv[rows, hd * HEAD_DIM:(hd + 1) * HEAD_DIM]
            kt_ref[b, hd] = _rms(kh, kn).T.astype(BF16)
            v_ref[b, hd, :, :HEAD_DIM] = kv[rows, KV_WIDTH + hd * HEAD_DIM:KV_WIDTH + (hd + 1) * HEAD_DIM].astype(BF16)
            v_ref[b, hd, :, HEAD_DIM:] = _ones_column(n_ctx)


def _ctxkv(ctx2, sc1, sh1, w_in, k_norm, bsz, n_ctx):
    d = ctx2.shape[1]
    kvw = 2 * KV_WIDTH
    return pl.pallas_call(
        functools.partial(_ctxkv_kernel, bsz=bsz, n_ctx=n_ctx),
        grid=(1,),
        in_specs=[
            pl.BlockSpec((bsz * n_ctx, d), lambda i: (0, 0)),
            pl.BlockSpec((MOD_ROWS, d), lambda i: (0, 0)),
            pl.BlockSpec((MOD_ROWS, d), lambda i: (0, 0)),
            pl.BlockSpec((1, d, kvw), lambda i: (0, 0, K_OFF // kvw)),
            pl.BlockSpec((1, HEAD_DIM), lambda i: (0, 0)),
        ],
        out_specs=[
            pl.BlockSpec((bsz, N_KV_HEADS, HEAD_DIM, n_ctx), lambda i: (0, 0, 0, 0)),
            pl.BlockSpec((bsz, N_KV_HEADS, n_ctx, V_EXT), lambda i: (0, 0, 0, 0)),
        ],
        out_shape=[
            jax.ShapeDtypeStruct((bsz, N_KV_HEADS, HEAD_DIM, n_ctx), BF16),
            jax.ShapeDtypeStruct((bsz, N_KV_HEADS, n_ctx, V_EXT), BF16),
        ],
        compiler_params=_cparams(("arbitrary",), 2 * (bsz * n_ctx * d * 4 + d * kvw * 4) + (8 << 20)),
        name="ctxkv",
    )(ctx2, sc1, sh1, w_in, k_norm)


def _pool_kernel(up_ref, uc_ref, un_ref, w_ref, s_ref, o_ref, ext_ref, *, tiles_per_batch, tm, n_tok):
    it = pl.program_id(0) % tiles_per_batch
    ext_ref[0:HALO, :] = jnp.where(it == 0, 0.0, up_ref[...])
    ext_ref[HALO:HALO + tm, :] = uc_ref[...]
    ext_ref[HALO + tm:HALO + tm + HALO, :] = jnp.where(it == tiles_per_batch - 1, 0.0, un_ref[...])
    t = it * tm + lax.broadcasted_iota(jnp.int32, (tm, POOL_GROUP), 0)
    for gi, w in enumerate(POOL_WINDOWS):
        half = w // 2
        cols = slice(gi * POOL_GROUP, (gi + 1) * POOL_GROUP)
        acc = ext_ref[HALO - half:HALO - half + tm, cols]
        for dd in range(-half + 1, half):
            acc = acc + ext_ref[HALO + dd:HALO + dd + tm, cols]
        cnt = (jnp.minimum(t + half, n_tok) - jnp.maximum(t - half, 0)).astype(F32)
        diff = acc / cnt - uc_ref[:, cols]
        mixed = jnp.dot(diff, w_ref[0, gi], preferred_element_type=F32) * s_ref[:, cols]
        o_ref[:, cols] = mixed.astype(BF16)


def _pool(u, pool_w, pool_scale, bsz, n_tok):
    tm = POOL_TM
    tpb = n_tok // tm
    n_rows = bsz * n_tok
    hb = tm // HALO
    last_hblk = n_rows // HALO - 1
    return pl.pallas_call(
        functools.partial(_pool_kernel, tiles_per_batch=tpb, tm=tm, n_tok=n_tok),
        grid=(bsz * tpb,),
        in_specs=[
            pl.BlockSpec((HALO, POOL_WIDTH), lambda i: (jnp.maximum(i * hb - 1, 0), 0)),
            pl.BlockSpec((tm, POOL_WIDTH), lambda i: (i, 0)),
            pl.BlockSpec((HALO, POOL_WIDTH), lambda i: (jnp.minimum((i + 1) * hb, last_hblk), 0)),
            pl.BlockSpec((1, len(POOL_WINDOWS), POOL_GROUP, POOL_GROUP), lambda i: (0, 0, 0, 0)),
            pl.BlockSpec((1, POOL_WIDTH), lambda i: (0, 0)),
        ],
        out_specs=pl.BlockSpec((tm, POOL_WIDTH), lambda i: (i, 0)),
        out_shape=jax.ShapeDtypeStruct((n_rows, POOL_WIDTH), BF16),
        scratch_shapes=[pltpu.VMEM((tm + 2 * HALO, POOL_WIDTH), F32)],
        compiler_params=_cparams(("arbitrary",), 6 * tm * POOL_WIDTH * 4 + (8 << 20)),
        name="pool",
    )(u, u, u, pool_w, pool_scale)


def _attn_kernel(q_ref, kt_ref, v_ref, kct_ref, vc_ref, o_ref, *, tq, tk, n_tok):
    rows = Q_PER_KV * tq
    q = q_ref[0].reshape(rows, HEAD_DIM)

    def step(carry, kt_c, v_c):
        m, acc = carry
        s = jnp.dot(q, kt_c, preferred_element_type=F32)
        m_new = jnp.maximum(m, jnp.max(s, axis=-1, keepdims=True))
        alpha = jnp.exp2(m - m_new)
        p = jnp.exp2(s - m_new)
        acc = alpha * acc + jnp.dot(p.astype(BF16), v_c, preferred_element_type=F32)
        return m_new, acc

    carry = (jnp.full((rows, 1), -jnp.inf, F32), jnp.zeros((rows, V_EXT), F32))
    carry = step(carry, kct_ref[0, 0], vc_ref[0, 0])
    for j in range(n_tok // tk):
        carry = step(carry, kt_ref[0, 0, :, j * tk:(j + 1) * tk], v_ref[0, 0, j * tk:(j + 1) * tk, :])
    _, acc = carry
    out = acc[:, :HEAD_DIM] / acc[:, HEAD_DIM:HEAD_DIM + 1]
    for g in range(Q_PER_KV):
        o_ref[:, g * HEAD_DIM:(g + 1) * HEAD_DIM] = out[g * tq:(g + 1) * tq].astype(BF16)


def _attn(q, kt, v, kct, vc, bsz, n_tok, n_ctx):
    tq, tk = ATTN_TQ, ATTN_TK
    nq = n_tok // tq
    gw = Q_PER_KV * HEAD_DIM
    return pl.pallas_call(
        functools.partial(_attn_kernel, tq=tq, tk=tk, n_tok=n_tok),
        grid=(bsz, N_KV_HEADS, nq),
        in_specs=[
            pl.BlockSpec((1, Q_PER_KV, tq, HEAD_DIM), lambda b, h, i: (b, h, i, 0)),
            pl.BlockSpec((1, 1, HEAD_DIM, n_tok), lambda b, h, i: (b, h, 0, 0)),
            pl.BlockSpec((1, 1, n_tok, V_EXT), lambda b, h, i: (b, h, 0, 0)),
            pl.BlockSpec((1, 1, HEAD_DIM, n_ctx), lambda b, h, i: (b, h, 0, 0)),
            pl.BlockSpec((1, 1, n_ctx, V_EXT), lambda b, h, i: (b, h, 0, 0)),
        ],
        out_specs=pl.BlockSpec((tq, gw), lambda b, h, i: (b * nq + i, h)),
        out_shape=jax.ShapeDtypeStruct((bsz * n_tok, ATTN_WIDTH), BF16),
        compiler_params=_cparams(("arbitrary", "arbitrary", "arbitrary"), 40 << 20),
        name="attn",
    )(q, kt, v, kct, vc)


def _layer_norm(r, g, b):
    mu = jnp.mean(r, axis=-1, keepdims=True)
    rc = r - mu
    var = jnp.mean(rc * rc, axis=-1, keepdims=True)
    return rc * lax.rsqrt(var + LN_EPS) * g + b


def _split_bf16(x):
    hi = x.astype(BF16)
    lo = (x - hi.astype(F32)).astype(BF16)
    return hi, lo


def _outproj_kernel(pool_ref, attn_ref, x_ref, w_ref, g1_ref, sc_ref, sh_ref, lg_ref, lb_ref, rw_ref,
                    acc_ref, h_ref, *, tiles_per_batch, sub):
    b = pl.program_id(0) // tiles_per_batch
    g1 = g1_ref[pl.ds(b, 1), :]
    sc = 1.0 + sc_ref[pl.ds(b, 1), :]
    sh = sh_ref[pl.ds(b, 1), :]
    rh, rl = _split_bf16(rw_ref[...])
    d = x_ref.shape[1]
    for r0 in range(0, x_ref.shape[0], sub):
        rows = slice(r0, r0 + sub)
        y = (jnp.dot(pool_ref[rows, :].astype(F32), w_ref[0, :POOL_WIDTH, :], preferred_element_type=F32)
             + jnp.dot(attn_ref[rows, :].astype(F32), w_ref[0, POOL_WIDTH:, :], preferred_element_type=F32))
        r = ALPHA * x_ref[rows, :] + g1 * y
        xm = _layer_norm(r, lg_ref[...], lb_ref[...])
        acc_ref[rows, :] = ALPHA * xm
        h2 = xm * sc + sh
        h_ref[rows, :d] = h2
        hh, hl = _split_bf16(h2)
        logits = (jnp.dot(hh, rh, preferred_element_type=F32) + jnp.dot(hl, rh, preferred_element_type=F32)
                  + jnp.dot(hh, rl, preferred_element_type=F32))
        valid = lax.broadcasted_iota(jnp.int32, logits.shape, 1) < N_EXPERTS
        lg = jnp.where(valid, logits, -jnp.inf)
        ex = jnp.exp(lg - jnp.max(lg, axis=-1, keepdims=True))
        h_ref[rows, d:] = ex / jnp.sum(ex, axis=-1, keepdims=True)


def _outproj(pool, attn, x2, w_out, g1, sc2, sh2, ln_g, ln_b, rw_pad, bsz, n_tok):
    tm = OUTPROJ_TM
    tpb = n_tok // tm
    d = x2.shape[1]
    n_rows = bsz * n_tok
    est = d * d * 4 + 2 * tm * d * 4 * 4 + 6 * tm * d * 4 + (8 << 20)
    return pl.pallas_call(
        functools.partial(_outproj_kernel, tiles_per_batch=tpb, sub=SUB_ROWS),
        grid=(bsz * tpb,),
        in_specs=[
            pl.BlockSpec((tm, POOL_WIDTH), lambda i: (i, 0)),
            pl.BlockSpec((tm, ATTN_WIDTH), lambda i: (i, 0)),
            pl.BlockSpec((tm, d), lambda i: (i, 0)),
            pl.BlockSpec((1, d, d), lambda i: (0, 0, 0)),
            pl.BlockSpec((MOD_ROWS, d), lambda i: (0, 0)),
            pl.BlockSpec((MOD_ROWS, d), lambda i: (0, 0)),
            pl.BlockSpec((MOD_ROWS, d), lambda i: (0, 0)),
            pl.BlockSpec((1, d), lambda i: (0, 0)),
            pl.BlockSpec((1, d), lambda i: (0, 0)),
            pl.BlockSpec((d, LANES), lambda i: (0, 0)),
        ],
        out_specs=[
            pl.BlockSpec((tm, d), lambda i: (i, 0)),
            pl.BlockSpec((tm, d + LANES), lambda i: (i, 0)),
        ],
        out_shape=[
            jax.ShapeDtypeStruct((n_rows, d), F32),
            jax.ShapeDtypeStruct((n_rows, d + LANES), F32),
        ],
        compiler_params=_cparams(("arbitrary",), est),
        name="outproj",
    )(pool, attn, x2, w_out, g1, sc2, sh2, ln_g, ln_b, rw_pad)


def _prefix_incl(flags_bf16, out_ref, n_tok):
    blk = PREFIX_BLK
    tri = (lax.broadcasted_iota(jnp.int32, (blk, blk), 0)
           >= lax.broadcasted_iota(jnp.int32, (blk, blk), 1)).astype(BF16)
    carry = jnp.zeros((1, LANES), F32)
    for k in range(n_tok // blk):
        p = jnp.dot(tri, flags_bf16[k * blk:(k + 1) * blk], preferred_element_type=F32) + carry
        out_ref[k * blk:(k + 1) * blk, :] = p
        carry = p[blk - 1:blk, :]


def _topk_kernel(aff_ref, idx_ref, cnt_ref, part_ref, *, n_tok, cap):
    lane = lax.broadcasted_iota(jnp.int32, (1, LANES), 1)
    bits = jnp.where(lane < N_EXPERTS, pltpu.bitcast(aff_ref[...], jnp.int32), -1)

    seg = n_tok // COUNT_CHAINS

    def search(_, carry):
        lo, hi = carry
        mid = lo + ((hi - lo + 1) >> 1)
        n_ge = sum(jnp.sum((bits[k * seg:(k + 1) * seg] >= mid).astype(jnp.int32), axis=0, keepdims=True)
                   for k in range(COUNT_CHAINS))
        ok = n_ge >= cap
        return jnp.where(ok, mid, lo), jnp.where(ok, hi, mid - 1)

    lo0 = jnp.zeros((1, LANES), jnp.int32)
    hi0 = jnp.full((1, LANES), 0x7F800000, jnp.int32)
    thr, _ = lax.fori_loop(0, 32, search, (lo0, hi0))

    gt = bits > thr
    eq = bits == thr
    need = (cap - jnp.sum(gt.astype(jnp.int32), axis=0, keepdims=True)).astype(F32)
    _prefix_incl(jnp.where(eq, 1.0, 0.0).astype(BF16), cnt_ref, n_tok)
    sel = gt | (eq & (cnt_ref[...] <= need))
    _prefix_incl(jnp.where(sel, 1.0, 0.0).astype(BF16), cnt_ref, n_tok)

    slot = lax.broadcasted_iota(jnp.int32, (1, cap), 1).astype(F32)
    blk = PREFIX_BLK
    part_ref[...] = jnp.zeros_like(part_ref)

    def count(k, _):
        r0 = pl.multiple_of(k * blk, blk)
        cblk = cnt_ref[pl.ds(r0, blk), :]
        for e in range(N_EXPERTS):
            hit = (cblk[:, e:e + 1] <= slot).astype(F32)
            part_ref[e] += jnp.sum(hit.reshape(blk // SUBLANES, SUBLANES, cap), axis=0)
        return 0

    lax.fori_loop(0, n_tok // blk, count, 0)
    for e in range(N_EXPERTS):
        idx_ref[0, e:e + 1, :] = jnp.sum(part_ref[e], axis=0, keepdims=True).astype(jnp.int32)


def _topk(h2ext, bsz, n_tok, cap):
    aff_blk = h2ext.shape[1] // LANES - 1
    return pl.pallas_call(
        functools.partial(_topk_kernel, n_tok=n_tok, cap=cap),
        grid=(bsz,),
        in_specs=[pl.BlockSpec((n_tok, LANES), lambda b: (b, aff_blk))],
        out_specs=pl.BlockSpec((1, N_EXPERTS, cap), lambda b: (b, 0, 0)),
        out_shape=jax.ShapeDtypeStruct((bsz, N_EXPERTS, cap), jnp.int32),
        scratch_shapes=[pltpu.VMEM((n_tok, LANES), F32), pltpu.VMEM((N_EXPERTS, SUBLANES, cap), F32)],
        compiler_params=_cparams(("arbitrary",), 32 << 20),
        name="topk",
    )(h2ext)


def _moe_kernel(idxp_ref, idxc_ref, idxn_ref, h_hbm, acc_in_hbm, wg_ref, wu_ref, wd_ref, g2_ref, acc_hbm,
                xe_ref, xb_ref, gate_ref, y_ref, ab_ref, sem, *, n_f, n_rows, d, cap, n_x, n_sc):
    del acc_in_hbm
    e = pl.program_id(0)
    f = pl.program_id(1)
    n_e = pl.num_programs(0)
    sem_x, sem_a, sem_s = sem.at[0], sem.at[1], sem.at[2]
    rx = n_rows // n_x
    rs = n_rows // n_sc

    def x_copy(idx_ref, g0, n, k):
        return pltpu.make_async_copy(h_hbm.at[pl.ds(idx_ref[0, 0, g0 + k], 1), :],
                                     xe_ref.at[pl.ds(g0, n), :].at[pl.ds(k, 1), :], sem_x)

    def a_copy(idx_ref, g0, n, k):
        return pltpu.make_async_copy(acc_hbm.at[pl.ds(idx_ref[0, 0, g0 + k], 1), :],
                                     ab_ref.at[pl.ds(g0, n), :].at[pl.ds(k, 1), :], sem_a)

    def s_copy(idx_ref, g0, n, k):
        return pltpu.make_async_copy(ab_ref.at[pl.ds(g0, n), :].at[pl.ds(k, 1), :],
                                     acc_hbm.at[pl.ds(idx_ref[0, 0, g0 + k], 1), :], sem_s)

    def start_window(make, idx_ref, j0, n):
        g0 = pl.multiple_of(j0, SUBLANES)
        for k in range(n):
            make(idx_ref, g0, n, k).start()

    def start_all(make, idx_ref):
        def body(g, _):
            start_window(make, idx_ref, g * SUBLANES, SUBLANES)
            return 0

        lax.fori_loop(0, n_rows // SUBLANES, body, 0)

    def wait_x():
        pltpu.make_async_copy(h_hbm.at[pl.ds(0, n_rows), :], xe_ref, sem_x).wait()

    def wait_a():
        pltpu.make_async_copy(acc_hbm.at[pl.ds(0, n_rows), :], ab_ref, sem_a).wait()

    def wait_s():
        pltpu.make_async_copy(ab_ref, acc_hbm.at[pl.ds(0, n_rows), :], sem_s).wait()

    @pl.when((e == 0) & (f == 0))
    def _():
        start_all(x_copy, idxc_ref)
        start_all(a_copy, idxc_ref)
        wait_a()

    @pl.when(f == 0)
    def _():
        wait_x()
        xb_ref[...] = xe_ref[:, :d].astype(BF16)
        lane = lax.broadcasted_iota(jnp.int32, (n_rows, LANES), 1)
        gate_ref[...] = jnp.sum(jnp.where(lane == e, xe_ref[:, d:], 0.0), axis=-1, keepdims=True)
        y_ref[...] = jnp.zeros_like(y_ref)

    @pl.when(f == n_sc + 1)
    def _():
        wait_s()

    def compute():
        x = xb_ref[...]
        a = jnp.dot(x, wg_ref[0, 0].astype(BF16), preferred_element_type=F32)
        b = jnp.dot(x, wu_ref[0, 0].astype(BF16), preferred_element_type=F32)
        hm = (a * _sigmoid(a) * b).astype(BF16)
        y_ref[...] += jnp.dot(hm, wd_ref[0, 0].astype(BF16), preferred_element_type=F32)

    def step_plain():
        compute()

    def step_scatter():
        start_window(x_copy, idxn_ref, (f - 1) * rx, rx)
        start_window(s_copy, idxp_ref, (f - 1) * rs, rs)
        compute()

    def step_gather():
        start_window(x_copy, idxn_ref, (f - 1) * rx, rx)
        start_window(a_copy, idxc_ref, (f - n_sc - 1) * rs, rs)
        compute()

    phase = jnp.where((f >= 1) & (f <= n_sc), 1, jnp.where((f > n_sc) & (f <= 2 * n_sc), 2, 0))
    lax.switch(phase, [step_plain, step_scatter, step_gather])

    @pl.when(f == n_f - 1)
    def _():
        wait_a()
        for bi in range(n_rows // cap):
            rows = slice(bi * cap, (bi + 1) * cap)
            ab_ref[rows, :] += (gate_ref[rows, :] * g2_ref[bi:bi + 1, :]) * y_ref[rows, :]

        @pl.when(e == n_e - 1)
        def _():
            wait_x()
            start_all(s_copy, idxc_ref)
            wait_s()


def _moe(idx_rows, h2ext, acc0, w_gate, w_up, w_down, g2, cap):
    d = acc0.shape[1]
    n_rows = idx_rows.shape[2]
    ff = w_gate.shape[3]
    tf = MOE_TF
    n_f = ff // tf
    n_x, n_sc = MOE_XGATHER_STEPS, MOE_SCATTER_STEPS
    assert n_x == 2 * n_sc and n_x + 1 <= n_f - 1 and n_rows % (n_x * SUBLANES) == 0
    est = (2 * 3 * d * tf * 4 + n_rows * h2ext.shape[1] * 4 + n_rows * d * 2 + n_rows * LANES * 4
           + 2 * n_rows * d * 4 + 3 * d * tf * 2 + 4 * n_rows * tf * 4 + (6 << 20))
    last = N_EXPERTS - 1
    return pl.pallas_call(
        functools.partial(_moe_kernel, n_f=n_f, n_rows=n_rows, d=d, cap=cap, n_x=n_x, n_sc=n_sc),
        grid=(N_EXPERTS, n_f),
        in_specs=[
            pl.BlockSpec((1, 1, n_rows), lambda e, f: (jnp.maximum(e - 1, 0), 0, 0), memory_space=pltpu.SMEM),
            pl.BlockSpec((1, 1, n_rows), lambda e, f: (e, 0, 0), memory_space=pltpu.SMEM),
            pl.BlockSpec((1, 1, n_rows), lambda e, f: (jnp.minimum(e + 1, last), 0, 0), memory_space=pltpu.SMEM),
            pl.BlockSpec(memory_space=pl.ANY),
            pl.BlockSpec(memory_space=pl.ANY),
            pl.BlockSpec((1, 1, d, tf), lambda e, f: (0, e, 0, f)),
            pl.BlockSpec((1, 1, d, tf), lambda e, f: (0, e, 0, f)),
            pl.BlockSpec((1, 1, tf, d), lambda e, f: (0, e, f, 0)),
            pl.BlockSpec((MOD_ROWS, d), lambda e, f: (0, 0)),
        ],
        out_specs=pl.BlockSpec(memory_space=pl.ANY),
        out_shape=jax.ShapeDtypeStruct(acc0.shape, F32),
        scratch_shapes=[
            pltpu.VMEM((n_rows, h2ext.shape[1]), F32),
            pltpu.VMEM((n_rows, d), BF16),
            pltpu.VMEM((n_rows, 1), F32),
            pltpu.VMEM((n_rows, d), F32),
            pltpu.VMEM((n_rows, d), F32),
            pltpu.SemaphoreType.DMA((3,)),
        ],
        input_output_aliases={4: 0},
        compiler_params=_cparams(("arbitrary", "arbitrary"), est),
        name="moe",
    )(idx_rows, idx_rows, idx_rows, h2ext, acc0, w_gate, w_up, w_down, g2)


def _final_kernel(a_ref, g_ref, b_ref, o_ref):
    o_ref[...] = _layer_norm(a_ref[...], g_ref[...], b_ref[...])


def _final(acc, ln_g, ln_b):
    n_rows, d = acc.shape
    tm = FINAL_TM
    return pl.pallas_call(
        _final_kernel,
        grid=(n_rows // tm,),
        in_specs=[
            pl.BlockSpec((tm, d), lambda i: (i, 0)),
            pl.BlockSpec((1, d), lambda i: (0, 0)),
            pl.BlockSpec((1, d), lambda i: (0, 0)),
        ],
        out_specs=pl.BlockSpec((tm, d), lambda i: (i, 0)),
        out_shape=jax.ShapeDtypeStruct((n_rows, d), F32),
        compiler_params=_cparams(("arbitrary",), 8 * tm * d * 4 + (8 << 20)),
        name="final_ln",
    )(acc, ln_g, ln_b)


def _rope_tables(n_tok):
    rows = n_tok // GRID_W
    row = jnp.broadcast_to(jnp.arange(rows, dtype=F32)[:, None], (rows, GRID_W)).reshape(-1)
    col = jnp.broadcast_to(jnp.arange(GRID_W, dtype=F32)[None, :], (rows, GRID_W)).reshape(-1)
    inv_freq = ROPE_THETA ** (-jnp.arange(0, ROPE_AXIS_DIM, 2, dtype=F32) / ROPE_AXIS_DIM)
    ang_r = row[:, None] * inv_freq[None, :]
    ang_c = col[:, None] * inv_freq[None, :]
    ang = jnp.concatenate([ang_r, ang_r, ang_c, ang_c], axis=-1)
    cos, sin = jnp.cos(ang), jnp.sin(ang)
    low = (np.arange(HEAD_DIM) % ROPE_AXIS_DIM) < ROPE_AXIS_DIM // 2
    sin_lo = jnp.where(low[None, :], -sin, 0.0)
    sin_hi = jnp.where(low[None, :], 0.0, sin)
    return cos, sin_lo, sin_hi


def kernel(x, c, ctx, c_ctx, ada_w, ada_b, w_in, q_norm, k_norm, pool_w, pool_scale, w_out,
           ln1_g, ln1_b, router_w, w_gate, w_up, w_down, ln2_g, ln2_b):
    bsz, n_tok, d = x.shape
    n_ctx = ctx.shape[1]
    assert ada_w.shape[0] == DEPTH and d == D_MODEL and bsz + 1 <= MOD_ROWS
    cap = CAPACITY_FACTOR * n_tok // N_EXPERTS

    c8 = jnp.concatenate([c, c_ctx[None, :], jnp.zeros((MOD_ROWS - bsz - 1, d), F32)], axis=0)
    mod = _ada(c8, ada_w, ada_b)
    sh1, sc1, g1, sh2, sc2, g2 = [mod[:, k * d:(k + 1) * d] for k in range(6)]

    x2 = x.reshape(bsz * n_tok, d)
    cos, sin_lo, sin_hi = _rope_tables(n_tok)
    u, q, kt, v = _inproj(x2, sc1, sh1, w_in, q_norm, k_norm, cos, sin_lo, sin_hi, bsz, n_tok)
    kct, vc = _ctxkv(ctx.reshape(bsz * n_ctx, d), sc1, sh1, w_in, k_norm, bsz, n_ctx)

    pool = _pool(u, pool_w, pool_scale, bsz, n_tok)
    attn = _attn(q, kt, v, kct, vc, bsz, n_tok, n_ctx)

    rw_pad = jnp.pad(router_w[0], ((0, 0), (0, LANES - N_EXPERTS)))
    acc0, h2ext = _outproj(pool, attn, x2, w_out, g1, sc2, sh2, ln1_g, ln1_b, rw_pad, bsz, n_tok)

    idx = _topk(h2ext, bsz, n_tok, cap)
    idx_rows = (idx + (jnp.arange(bsz, dtype=jnp.int32) * n_tok)[:, None, None])
    idx_rows = idx_rows.transpose(1, 0, 2).reshape(N_EXPERTS, 1, bsz * cap)

    acc = _moe(idx_rows, h2ext, acc0, w_gate, w_up, w_down, g2, cap)
    out = _final(acc, ln2_g, ln2_b)
    return out.reshape(bsz, n_tok, d)
```

```python
import functools

import jax
import jax.numpy as jnp
import numpy as np
from jax import lax
from jax.experimental import pallas as pl
from jax.experimental.pallas import tpu as pltpu

F32 = jnp.float32
BF16 = jnp.bfloat16

D_MODEL = 2048
GRID_W = 64
HEAD_DIM = 128
ATTN_WIDTH = D_MODEL // 2
N_HEADS = ATTN_WIDTH // HEAD_DIM
N_KV_HEADS = 2
Q_PER_KV = N_HEADS // N_KV_HEADS
KV_WIDTH = N_KV_HEADS * HEAD_DIM
POOL_WIDTH = D_MODEL - ATTN_WIDTH
POOL_WINDOWS = (2, 4, 8, 16)
POOL_GROUP = POOL_WIDTH // len(POOL_WINDOWS)
Q_OFF = POOL_WIDTH
K_OFF = Q_OFF + ATTN_WIDTH
V_OFF = K_OFF + KV_WIDTH
IN_WIDTH = V_OFF + KV_WIDTH
ROPE_AXIS_DIM = HEAD_DIM // 2
ROPE_THETA = 10000.0
ATTN_SCALE = HEAD_DIM ** -0.5
Q_SCALE = ATTN_SCALE * float(np.log2(np.e))
V_EXT = 2 * HEAD_DIM
N_EXPERTS = 16
CAPACITY_FACTOR = 2
LN_EPS = 1e-6
QK_EPS = 1e-6
DEPTH = 1
ALPHA = (2.0 * DEPTH) ** 0.25

LANES = 128
SUBLANES = 8
V7X_VMEM_BYTES = 64 * 1024 * 1024
MOD_ROWS = SUBLANES
HALO = SUBLANES

ADA_TN = 1024
SUB_ROWS = 256
INPROJ_TM = 512
POOL_TM = 512
ATTN_TQ = 512
ATTN_TK = 512
OUTPROJ_TM = 512
MOE_TF = 512
MOE_XGATHER_STEPS = 8
MOE_SCATTER_STEPS = 4
FINAL_TM = 512
PREFIX_BLK = 256
COUNT_CHAINS = 8


def _vmem_limit(nbytes):
    return int(min(nbytes, V7X_VMEM_BYTES - 4 * 1024 * 1024))


def _cparams(sem, nbytes):
    return pltpu.CompilerParams(dimension_semantics=sem, vmem_limit_bytes=_vmem_limit(nbytes))


def _sigmoid(x):
    return 1.0 / (1.0 + jnp.exp(-x))


def _ada_kernel(c_ref, w_ref, b_ref, o_ref):
    cv = c_ref[...]
    s = cv * _sigmoid(cv)
    o_ref[...] = jnp.dot(s, w_ref[0], preferred_element_type=F32) + b_ref[...]


def _ada(c8, ada_w, ada_b):
    d = c8.shape[1]
    n_out = ada_w.shape[2]
    return pl.pallas_call(
        _ada_kernel,
        grid=(n_out // ADA_TN,),
        in_specs=[
            pl.BlockSpec((MOD_ROWS, d), lambda j: (0, 0)),
            pl.BlockSpec((1, d, ADA_TN), lambda j: (0, 0, j)),
            pl.BlockSpec((1, ADA_TN), lambda j: (0, j)),
        ],
        out_specs=pl.BlockSpec((MOD_ROWS, ADA_TN), lambda j: (0, j)),
        out_shape=jax.ShapeDtypeStruct((MOD_ROWS, n_out), F32),
        compiler_params=_cparams(("arbitrary",), 2 * d * ADA_TN * 4 + (8 << 20)),
        name="ada",
    )(c8, ada_w, ada_b)


def _ones_column(n):
    lane = lax.broadcasted_iota(jnp.int32, (n, V_EXT - HEAD_DIM), 1)
    return jnp.where(lane == 0, 1.0, 0.0).astype(BF16)


def _rms(xh, g):
    ms = jnp.mean(xh * xh, axis=-1, keepdims=True)
    return xh * lax.rsqrt(ms + QK_EPS) * g


def _rope(xn, cos, sin_lo, sin_hi):
    return (xn * cos + pltpu.roll(xn, HEAD_DIM - ROPE_AXIS_DIM // 2, 1) * sin_lo
            + pltpu.roll(xn, ROPE_AXIS_DIM // 2, 1) * sin_hi)


def _inproj_kernel(x_ref, sc_ref, sh_ref, w_ref, qn_ref, kn_ref, cos_ref, slo_ref, shi_ref,
                   u_ref, q_ref, kt_ref, v_ref, *, tiles_per_batch, sub):
    b = pl.program_id(0) // tiles_per_batch
    sc = 1.0 + sc_ref[pl.ds(b, 1), :]
    sh = sh_ref[pl.ds(b, 1), :]
    qn = qn_ref[...]
    kn = kn_ref[...]
    for r0 in range(0, x_ref.shape[0], sub):
        rows = slice(r0, r0 + sub)
        h = x_ref[rows, :] * sc + sh
        proj = jnp.dot(h, w_ref[0], preferred_element_type=F32)
        u_ref[rows, :] = proj[:, :Q_OFF]
        cos = cos_ref[rows, :]
        slo = slo_ref[rows, :]
        shi = shi_ref[rows, :]
        for hd in range(N_HEADS):
            xh = proj[:, Q_OFF + hd * HEAD_DIM:Q_OFF + (hd + 1) * HEAD_DIM]
            qr = _rope(_rms(xh, qn), cos, slo, shi) * Q_SCALE
            q_ref[0, hd, rows, :] = qr.astype(BF16)
        for hd in range(N_KV_HEADS):
            kh = proj[:, K_OFF + hd * HEAD_DIM:K_OFF + (hd + 1) * HEAD_DIM]
            kr = _rope(_rms(kh, kn), cos, slo, shi)
            kt_ref[0, hd, :, rows] = kr.T.astype(BF16)
            v_ref[0, hd, rows, :HEAD_DIM] = proj[:, V_OFF + hd * HEAD_DIM:V_OFF + (hd + 1) * HEAD_DIM].astype(BF16)
            v_ref[0, hd, rows, HEAD_DIM:] = _ones_column(sub)


def _inproj(x2, sc1, sh1, w_in, q_norm, k_norm, cos, sin_lo, sin_hi, bsz, n_tok):
    tm = INPROJ_TM
    tpb = n_tok // tm
    d = x2.shape[1]
    est = (2 * tm * d * 4 + d * IN_WIDTH * 4 + 2 * tm * IN_WIDTH * 4 + 2 * tm * IN_WIDTH * 4
           + (8 << 20))
    return pl.pallas_call(
        functools.partial(_inproj_kernel, tiles_per_batch=tpb, sub=SUB_ROWS),
        grid=(bsz * tpb,),
        in_specs=[
            pl.BlockSpec((tm, d), lambda i: (i, 0)),
            pl.BlockSpec((MOD_ROWS, d), lambda i: (0, 0)),
            pl.BlockSpec((MOD_ROWS, d), lambda i: (0, 0)),
            pl.BlockSpec((1, d, IN_WIDTH), lambda i: (0, 0, 0)),
            pl.BlockSpec((1, HEAD_DIM), lambda i: (0, 0)),
            pl.BlockSpec((1, HEAD_DIM), lambda i: (0, 0)),
            pl.BlockSpec((tm, HEAD_DIM), lambda i: (i % tpb, 0)),
            pl.BlockSpec((tm, HEAD_DIM), lambda i: (i % tpb, 0)),
            pl.BlockSpec((tm, HEAD_DIM), lambda i: (i % tpb, 0)),
        ],
        out_specs=[
            pl.BlockSpec((tm, POOL_WIDTH), lambda i: (i, 0)),
            pl.BlockSpec((1, N_HEADS, tm, HEAD_DIM), lambda i: (i // tpb, 0, i % tpb, 0)),
            pl.BlockSpec((1, N_KV_HEADS, HEAD_DIM, tm), lambda i: (i // tpb, 0, 0, i % tpb)),
            pl.BlockSpec((1, N_KV_HEADS, tm, V_EXT), lambda i: (i // tpb, 0, i % tpb, 0)),
        ],
        out_shape=[
            jax.ShapeDtypeStruct((bsz * n_tok, POOL_WIDTH), F32),
            jax.ShapeDtypeStruct((bsz, N_HEADS, n_tok, HEAD_DIM), BF16),
            jax.ShapeDtypeStruct((bsz, N_KV_HEADS, HEAD_DIM, n_tok), BF16),
            jax.ShapeDtypeStruct((bsz, N_KV_HEADS, n_tok, V_EXT), BF16),
        ],
        compiler_params=_cparams(("arbitrary",), est),
        name="inproj",
    )(x2, sc1, sh1, w_in, q_norm, k_norm, cos, sin_lo, sin_hi)


def _ctxkv_kernel(ctx_ref, sc_ref, sh_ref, w_ref, kn_ref, kt_ref, v_ref, *, bsz, n_ctx):
    hc = ctx_ref[...] * (1.0 + sc_ref[2:3, :]) + sh_ref[2:3, :]
    kv = jnp.dot(hc, w_ref[0], preferred_element_type=F32)
    kn = kn_ref[...]
    for b in range(bsz):
        rows = slice(b * n_ctx, (b + 1) * n_ctx)
        for hd in range(N_KV_HEADS):
            kh = kv[rows, hd * HEAD_DIM:(hd + 1) * HEAD_DIM]
            kt_ref[b, hd] = _rms(kh, kn).T.astype(BF16)
            v_ref[b, hd, :, :HEAD_DIM] = kv[rows, KV_WIDTH + hd * HEAD_DIM:KV_WIDTH + (hd + 1) * HEAD_DIM].astype(BF16)
            v_ref[b, hd, :, HEAD_DIM:] = _ones_column(n_ctx)


def _ctxkv(ctx2, sc1, sh1, w_in, k_norm, bsz, n_ctx):
    d = ctx2.shape[1]
    kvw = 2 * KV_WIDTH
    return pl.pallas_call(
        functools.partial(_ctxkv_kernel, bsz=bsz, n_ctx=n_ctx),
        grid=(1,),
        in_specs=[
            pl.BlockSpec((bsz * n_ctx, d), lambda i: (0, 0)),
            pl.BlockSpec((MOD_ROWS, d), lambda i: (0, 0)),
            pl.BlockSpec((MOD_ROWS, d), lambda i: (0, 0)),
            pl.BlockSpec((1, d, kvw), lambda i: (0, 0, K_OFF // kvw)),
            pl.BlockSpec((1, HEAD_DIM), lambda i: (0, 0)),
        ],
        out_specs=[
            pl.BlockSpec((bsz, N_KV_HEADS, HEAD_DIM, n_ctx), lambda i: (0, 0, 0, 0)),
            pl.BlockSpec((bsz, N_KV_HEADS, n_ctx, V_EXT), lambda i: (0, 0, 0, 0)),
        ],
        out_shape=[
            jax.ShapeDtypeStruct((bsz, N_KV_HEADS, HEAD_DIM, n_ctx), BF16),
            jax.ShapeDtypeStruct((bsz, N_KV_HEADS, n_ctx, V_EXT), BF16),
        ],
        compiler_params=_cparams(("arbitrary",), 2 * (bsz * n_ctx * d * 4 + d * kvw * 4) + (8 << 20)),
        name="ctxkv",
    )(ctx2, sc1, sh1, w_in, k_norm)


def _pool_kernel(up_ref, uc_ref, un_ref, w_ref, s_ref, o_ref, ext_ref, *, tiles_per_batch, tm, n_tok):
    it = pl.program_id(0) % tiles_per_batch
    ext_ref[0:HALO, :] = jnp.where(it == 0, 0.0, up_ref[...])
    ext_ref[HALO:HALO + tm, :] = uc_ref[...]
    ext_ref[HALO + tm:HALO + tm + HALO, :] = jnp.where(it == tiles_per_batch - 1, 0.0, un_ref[...])
    t = it * tm + lax.broadcasted_iota(jnp.int32, (tm, POOL_GROUP), 0)
    for gi, w in enumerate(POOL_WINDOWS):
        half = w // 2
        cols = slice(gi * POOL_GROUP, (gi + 1) * POOL_GROUP)
        acc = ext_ref[HALO - half:HALO - half + tm, cols]
        for dd in range(-half + 1, half):
            acc = acc + ext_ref[HALO + dd:HALO + dd + tm, cols]
        cnt = (jnp.minimum(t + half, n_tok) - jnp.maximum(t - half, 0)).astype(F32)
        diff = acc / cnt - uc_ref[:, cols]
        mixed = jnp.dot(diff, w_ref[0, gi], preferred_element_type=F32) * s_ref[:, cols]
        o_ref[:, cols] = mixed.astype(BF16)


def _pool(u, pool_w, pool_scale, bsz, n_tok):
    tm = POOL_TM
    tpb = n_tok // tm
    n_rows = bsz * n_tok
    hb = tm // HALO
    last_hblk = n_rows // HALO - 1
    return pl.pallas_call(
        functools.partial(_pool_kernel, tiles_per_batch=tpb, tm=tm, n_tok=n_tok),
        grid=(bsz * tpb,),
        in_specs=[
            pl.BlockSpec((HALO, POOL_WIDTH), lambda i: (jnp.maximum(i * hb - 1, 0), 0)),
            pl.BlockSpec((tm, POOL_WIDTH), lambda i: (i, 0)),
            pl.BlockSpec((HALO, POOL_WIDTH), lambda i: (jnp.minimum((i + 1) * hb, last_hblk), 0)),
            pl.BlockSpec((1, len(POOL_WINDOWS), POOL_GROUP, POOL_GROUP), lambda i: (0, 0, 0, 0)),
            pl.BlockSpec((1, POOL_WIDTH), lambda i: (0, 0)),
        ],
        out_specs=pl.BlockSpec((tm, POOL_WIDTH), lambda i: (i, 0)),
        out_shape=jax.ShapeDtypeStruct((n_rows, POOL_WIDTH), BF16),
        scratch_shapes=[pltpu.VMEM((tm + 2 * HALO, POOL_WIDTH), F32)],
        compiler_params=_cparams(("arbitrary",), 6 * tm * POOL_WIDTH * 4 + (8 << 20)),
        name="pool",
    )(u, u, u, pool_w, pool_scale)


def _attn_kernel(q_ref, kt_ref, v_ref, kct_ref, vc_ref, o_ref, *, tq, tk, n_tok):
    rows = Q_PER_KV * tq
    q = q_ref[0].reshape(rows, HEAD_DIM)

    def step(carry, kt_c, v_c):
        m, acc = carry
        s = jnp.dot(q, kt_c, preferred_element_type=F32)
        m_new = jnp.maximum(m, jnp.max(s, axis=-1, keepdims=True))
        alpha = jnp.exp2(m - m_new)
        p = jnp.exp2(s - m_new)
        acc = alpha * acc + jnp.dot(p.astype(BF16), v_c, preferred_element_type=F32)
        return m_new, acc

    carry = (jnp.full((rows, 1), -jnp.inf, F32), jnp.zeros((rows, V_EXT), F32))
    carry = step(carry, kct_ref[0, 0], vc_ref[0, 0])
    for j in range(n_tok // tk):
        carry = step(carry, kt_ref[0, 0, :, j * tk:(j + 1) * tk], v_ref[0, 0, j * tk:(j + 1) * tk, :])
    _, acc = carry
    out = acc[:, :HEAD_DIM] / acc[:, HEAD_DIM:HEAD_DIM + 1]
    for g in range(Q_PER_KV):
        o_ref[:, g * HEAD_DIM:(g + 1) * HEAD_DIM] = out[g * tq:(g + 1) * tq].astype(BF16)


def _attn(q, kt, v, kct, vc, bsz, n_tok, n_ctx):
    tq, tk = ATTN_TQ, ATTN_TK
    nq = n_tok // tq
    gw = Q_PER_KV * HEAD_DIM
    return pl.pallas_call(
        functools.partial(_attn_kernel, tq=tq, tk=tk, n_tok=n_tok),
        grid=(bsz, N_KV_HEADS, nq),
        in_specs=[
            pl.BlockSpec((1, Q_PER_KV, tq, HEAD_DIM), lambda b, h, i: (b, h, i, 0)),
            pl.BlockSpec((1, 1, HEAD_DIM, n_tok), lambda b, h, i: (b, h, 0, 0)),
            pl.BlockSpec((1, 1, n_tok, V_EXT), lambda b, h, i: (b, h, 0, 0)),
            pl.BlockSpec((1, 1, HEAD_DIM, n_ctx), lambda b, h, i: (b, h, 0, 0)),
            pl.BlockSpec((1, 1, n_ctx, V_EXT), lambda b, h, i: (b, h, 0, 0)),
        ],
        out_specs=pl.BlockSpec((tq, gw), lambda b, h, i: (b * nq + i, h)),
        out_shape=jax.ShapeDtypeStruct((bsz * n_tok, ATTN_WIDTH), BF16),
        compiler_params=_cparams(("arbitrary", "arbitrary", "arbitrary"), 40 << 20),
        name="attn",
    )(q, kt, v, kct, vc)


def _layer_norm(r, g, b):
    mu = jnp.mean(r, axis=-1, keepdims=True)
    rc = r - mu
    var = jnp.mean(rc * rc, axis=-1, keepdims=True)
    return rc * lax.rsqrt(var + LN_EPS) * g + b


def _outproj_kernel(pool_ref, attn_ref, x_ref, w_ref, g1_ref, sc_ref, sh_ref, lg_ref, lb_ref, rw_ref,
                    acc_ref, h_ref, *, tiles_per_batch, sub):
    b = pl.program_id(0) // tiles_per_batch
    g1 = g1_ref[pl.ds(b, 1), :]
    sc = 1.0 + sc_ref[pl.ds(b, 1), :]
    sh = sh_ref[pl.ds(b, 1), :]
    rw = rw_ref[...].astype(BF16)
    d = x_ref.shape[1]
    for r0 in range(0, x_ref.shape[0], sub):
        rows = slice(r0, r0 + sub)
        y = (jnp.dot(pool_ref[rows, :].astype(F32), w_ref[0, :POOL_WIDTH, :], preferred_element_type=F32)
             + jnp.dot(attn_ref[rows, :].astype(F32), w_ref[0, POOL_WIDTH:, :], preferred_element_type=F32))
        r = ALPHA * x_ref[rows, :] + g1 * y
        xm = _layer_norm(r, lg_ref[...], lb_ref[...])
        acc_ref[rows, :] = ALPHA * xm
        h2 = xm * sc + sh
        h_ref[rows, :d] = h2
        logits = jnp.dot(h2.astype(BF16), rw, preferred_element_type=F32)
        valid = lax.broadcasted_iota(jnp.int32, logits.shape, 1) < N_EXPERTS
        lg = jnp.where(valid, logits, -jnp.inf)
        ex = jnp.exp(lg - jnp.max(lg, axis=-1, keepdims=True))
        h_ref[rows, d:] = ex / jnp.sum(ex, axis=-1, keepdims=True)


def _outproj(pool, attn, x2, w_out, g1, sc2, sh2, ln_g, ln_b, rw_pad, bsz, n_tok):
    tm = OUTPROJ_TM
    tpb = n_tok // tm
    d = x2.shape[1]
    n_rows = bsz * n_tok
    est = d * d * 4 + 2 * tm * d * 4 * 4 + 6 * tm * d * 4 + (8 << 20)
    return pl.pallas_call(
        functools.partial(_outproj_kernel, tiles_per_batch=tpb, sub=SUB_ROWS),
        grid=(bsz * tpb,),
        in_specs=[
            pl.BlockSpec((tm, POOL_WIDTH), lambda i: (i, 0)),
            pl.BlockSpec((tm, ATTN_WIDTH), lambda i: (i, 0)),
            pl.BlockSpec((tm, d), lambda i: (i, 0)),
            pl.BlockSpec((1, d, d), lambda i: (0, 0, 0)),
            pl.BlockSpec((MOD_ROWS, d), lambda i: (0, 0)),
            pl.BlockSpec((MOD_ROWS, d), lambda i: (0, 0)),
            pl.BlockSpec((MOD_ROWS, d), lambda i: (0, 0)),
            pl.BlockSpec((1, d), lambda i: (0, 0)),
            pl.BlockSpec((1, d), lambda i: (0, 0)),
            pl.BlockSpec((d, LANES), lambda i: (0, 0)),
        ],
        out_specs=[
            pl.BlockSpec((tm, d), lambda i: (i, 0)),
            pl.BlockSpec((tm, d + LANES), lambda i: (i, 0)),
        ],
        out_shape=[
            jax.ShapeDtypeStruct((n_rows, d), F32),
            jax.ShapeDtypeStruct((n_rows, d + LANES), F32),
        ],
        compiler_params=_cparams(("arbitrary",), est),
        name="outproj",
    )(pool, attn, x2, w_out, g1, sc2, sh2, ln_g, ln_b, rw_pad)


def _prefix_incl(flags_bf16, out_ref, n_tok):
    blk = PREFIX_BLK
    tri = (lax.broadcasted_iota(jnp.int32, (blk, blk), 0)
           >= lax.broadcasted_iota(jnp.int32, (blk, blk), 1)).astype(BF16)
    carry = jnp.zeros((1, LANES), F32)
    for k in range(n_tok // blk):
        p = jnp.dot(tri, flags_bf16[k * blk:(k + 1) * blk], preferred_element_type=F32) + carry
        out_ref[k * blk:(k + 1) * blk, :] = p
        carry = p[blk - 1:blk, :]


def _topk_kernel(aff_ref, idx_ref, cnt_ref, part_ref, *, n_tok, cap):
    lane = lax.broadcasted_iota(jnp.int32, (1, LANES), 1)
    bits = jnp.where(lane < N_EXPERTS, pltpu.bitcast(aff_ref[...], jnp.int32), -1)

    seg = n_tok // COUNT_CHAINS

    def search(_, carry):
        lo, hi = carry
        mid = lo + ((hi - lo + 1) >> 1)
        n_ge = sum(jnp.sum((bits[k * seg:(k + 1) * seg] >= mid).astype(jnp.int32), axis=0, keepdims=True)
                   for k in range(COUNT_CHAINS))
        ok = n_ge >= cap
        return jnp.where(ok, mid, lo), jnp.where(ok, hi, mid - 1)

    lo0 = jnp.zeros((1, LANES), jnp.int32)
    hi0 = jnp.full((1, LANES), 0x7F800000, jnp.int32)
    thr, _ = lax.fori_loop(0, 32, search, (lo0, hi0))

    gt = bits > thr
    eq = bits == thr
    need = (cap - jnp.sum(gt.astype(jnp.int32), axis=0, keepdims=True)).astype(F32)
    _prefix_incl(jnp.where(eq, 1.0, 0.0).astype(BF16), cnt_ref, n_tok)
    sel = gt | (eq & (cnt_ref[...] <= need))
    _prefix_incl(jnp.where(sel, 1.0, 0.0).astype(BF16), cnt_ref, n_tok)

    slot = lax.broadcasted_iota(jnp.int32, (1, cap), 1).astype(F32)
    blk = PREFIX_BLK
    part_ref[...] = jnp.zeros_like(part_ref)

    def count(k, _):
        r0 = pl.multiple_of(k * blk, blk)
        cblk = cnt_ref[pl.ds(r0, blk), :]
        for e in range(N_EXPERTS):
            hit = (cblk[:, e:e + 1] <= slot).astype(F32)
            part_ref[e] += jnp.sum(hit.reshape(blk // SUBLANES, SUBLANES, cap), axis=0)
        return 0

    lax.fori_loop(0, n_tok // blk, count, 0)
    for e in range(N_EXPERTS):
        idx_ref[0, e:e + 1, :] = jnp.sum(part_ref[e], axis=0, keepdims=True).astype(jnp.int32)


def _topk(h2ext, bsz, n_tok, cap):
    aff_blk = h2ext.shape[1] // LANES - 1
    return pl.pallas_call(
        functools.partial(_topk_kernel, n_tok=n_tok, cap=cap),
        grid=(bsz,),
        in_specs=[pl.BlockSpec((n_tok, LANES), lambda b: (b, aff_blk))],
        out_specs=pl.BlockSpec((1, N_EXPERTS, cap), lambda b: (b, 0, 0)),
        out_shape=jax.ShapeDtypeStruct((bsz, N_EXPERTS, cap), jnp.int32),
        scratch_shapes=[pltpu.VMEM((n_tok, LANES), F32), pltpu.VMEM((N_EXPERTS, SUBLANES, cap), F32)],
        compiler_params=_cparams(("arbitrary",), 32 << 20),
        name="topk",
    )(h2ext)


def _moe_kernel(idxp_ref, idxc_ref, idxn_ref, h_hbm, acc_in_hbm, wg_ref, wu_ref, wd_ref, g2_ref, acc_hbm,
                xe_ref, xb_ref, gate_ref, y_ref, ab_ref, sem, *, n_f, n_rows, d, cap, n_x, n_sc):
    del acc_in_hbm
    e = pl.program_id(0)
    f = pl.program_id(1)
    n_e = pl.num_programs(0)
    sem_x, sem_a, sem_s = sem.at[0], sem.at[1], sem.at[2]
    rx = n_rows // n_x
    rs = n_rows // n_sc

    def x_copy(idx_ref, g0, n, k):
        return pltpu.make_async_copy(h_hbm.at[pl.ds(idx_ref[0, 0, g0 + k], 1), :],
                                     xe_ref.at[pl.ds(g0, n), :].at[pl.ds(k, 1), :], sem_x)

    def a_copy(idx_ref, g0, n, k):
        return pltpu.make_async_copy(acc_hbm.at[pl.ds(idx_ref[0, 0, g0 + k], 1), :],
                                     ab_ref.at[pl.ds(g0, n), :].at[pl.ds(k, 1), :], sem_a)

    def s_copy(idx_ref, g0, n, k):
        return pltpu.make_async_copy(ab_ref.at[pl.ds(g0, n), :].at[pl.ds(k, 1), :],
                                     acc_hbm.at[pl.ds(idx_ref[0, 0, g0 + k], 1), :], sem_s)

    def start_window(make, idx_ref, j0, n):
        g0 = pl.multiple_of(j0, SUBLANES)
        for k in range(n):
            make(idx_ref, g0, n, k).start()

    def start_all(make, idx_ref):
        def body(g, _):
            start_window(make, idx_ref, g * SUBLANES, SUBLANES)
            return 0

        lax.fori_loop(0, n_rows // SUBLANES, body, 0)

    def wait_x():
        pltpu.make_async_copy(h_hbm.at[pl.ds(0, n_rows), :], xe_ref, sem_x).wait()

    def wait_a():
        pltpu.make_async_copy(acc_hbm.at[pl.ds(0, n_rows), :], ab_ref, sem_a).wait()

    def wait_s():
        pltpu.make_async_copy(ab_ref, acc_hbm.at[pl.ds(0, n_rows), :], sem_s).wait()

    @pl.when((e == 0) & (f == 0))
    def _():
        start_all(x_copy, idxc_ref)
        start_all(a_copy, idxc_ref)
        wait_a()

    @pl.when(f == 0)
    def _():
        wait_x()
        xb_ref[...] = xe_ref[:, :d].astype(BF16)
        lane = lax.broadcasted_iota(jnp.int32, (n_rows, LANES), 1)
        gate_ref[...] = jnp.sum(jnp.where(lane == e, xe_ref[:, d:], 0.0), axis=-1, keepdims=True)
        y_ref[...] = jnp.zeros_like(y_ref)

    @pl.when(f == n_sc + 1)
    def _():
        wait_s()

    def compute():
        x = xb_ref[...]
        a = jnp.dot(x, wg_ref[0, 0].astype(BF16), preferred_element_type=F32)
        b = jnp.dot(x, wu_ref[0, 0].astype(BF16), preferred_element_type=F32)
        hm = (a * _sigmoid(a) * b).astype(BF16)
        y_ref[...] += jnp.dot(hm, wd_ref[0, 0].astype(BF16), preferred_element_type=F32)

    def step_plain():
        compute()

    def step_scatter():
        start_window(x_copy, idxn_ref, (f - 1) * rx, rx)
        start_window(s_copy, idxp_ref, (f - 1) * rs, rs)
        compute()

    def step_gather():
        start_window(x_copy, idxn_ref, (f - 1) * rx, rx)
        start_window(a_copy, idxc_ref, (f - n_sc - 1) * rs, rs)
        compute()

    phase = jnp.where((f >= 1) & (f <= n_sc), 1, jnp.where((f > n_sc) & (f <= 2 * n_sc), 2, 0))
    lax.switch(phase, [step_plain, step_scatter, step_gather])

    @pl.when(f == n_f - 1)
    def _():
        wait_a()
        for bi in range(n_rows // cap):
            rows = slice(bi * cap, (bi + 1) * cap)
            ab_ref[rows, :] += (gate_ref[rows, :] * g2_ref[bi:bi + 1, :]) * y_ref[rows, :]

        @pl.when(e == n_e - 1)
        def _():
            wait_x()
            start_all(s_copy, idxc_ref)
            wait_s()


def _moe(idx_rows, h2ext, acc0, w_gate, w_up, w_down, g2, cap):
    d = acc0.shape[1]
    n_rows = idx_rows.shape[2]
    ff = w_gate.shape[3]
    tf = MOE_TF
    n_f = ff // tf
    n_x, n_sc = MOE_XGATHER_STEPS, MOE_SCATTER_STEPS
    assert n_x == 2 * n_sc and n_x + 1 <= n_f - 1 and n_rows % (n_x * SUBLANES) == 0
    est = (2 * 3 * d * tf * 4 + n_rows * h2ext.shape[1] * 4 + n_rows * d * 2 + n_rows * LANES * 4
           + 2 * n_rows * d * 4 + 3 * d * tf * 2 + 4 * n_rows * tf * 4 + (6 << 20))
    last = N_EXPERTS - 1
    return pl.pallas_call(
        functools.partial(_moe_kernel, n_f=n_f, n_rows=n_rows, d=d, cap=cap, n_x=n_x, n_sc=n_sc),
        grid=(N_EXPERTS, n_f),
        in_specs=[
            pl.BlockSpec((1, 1, n_rows), lambda e, f: (jnp.maximum(e - 1, 0), 0, 0), memory_space=pltpu.SMEM),
            pl.BlockSpec((1, 1, n_rows), lambda e, f: (e, 0, 0), memory_space=pltpu.SMEM),
            pl.BlockSpec((1, 1, n_rows), lambda e, f: (jnp.minimum(e + 1, last), 0, 0), memory_space=pltpu.SMEM),
            pl.BlockSpec(memory_space=pl.ANY),
            pl.BlockSpec(memory_space=pl.ANY),
            pl.BlockSpec((1, 1, d, tf), lambda e, f: (0, e, 0, f)),
            pl.BlockSpec((1, 1, d, tf), lambda e, f: (0, e, 0, f)),
            pl.BlockSpec((1, 1, tf, d), lambda e, f: (0, e, f, 0)),
            pl.BlockSpec((MOD_ROWS, d), lambda e, f: (0, 0)),
        ],
        out_specs=pl.BlockSpec(memory_space=pl.ANY),
        out_shape=jax.ShapeDtypeStruct(acc0.shape, F32),
        scratch_shapes=[
            pltpu.VMEM((n_rows, h2ext.shape[1]), F32),
            pltpu.VMEM((n_rows, d), BF16),
            pltpu.VMEM((n_rows, 1), F32),
            pltpu.VMEM((n_rows, d), F32),
            pltpu.VMEM((n_rows, d), F32),
            pltpu.SemaphoreType.DMA((3,)),
        ],
        input_output_aliases={4: 0},
        compiler_params=_cparams(("arbitrary", "arbitrary"), est),
        name="moe",
    )(idx_rows, idx_rows, idx_rows, h2ext, acc0, w_gate, w_up, w_down, g2)


def _final_kernel(a_ref, g_ref, b_ref, o_ref):
    o_ref[...] = _layer_norm(a_ref[...], g_ref[...], b_ref[...])


def _final(acc, ln_g, ln_b):
    n_rows, d = acc.shape
    tm = FINAL_TM
    return pl.pallas_call(
        _final_kernel,
        grid=(n_rows // tm,),
        in_specs=[
            pl.BlockSpec((tm, d), lambda i: (i, 0)),
            pl.BlockSpec((1, d), lambda i: (0, 0)),
            pl.BlockSpec((1, d), lambda i: (0, 0)),
        ],
        out_specs=pl.BlockSpec((tm, d), lambda i: (i, 0)),
        out_shape=jax.ShapeDtypeStruct((n_rows, d), F32),
        compiler_params=_cparams(("arbitrary",), 8 * tm * d * 4 + (8 << 20)),
        name="final_ln",
    )(acc, ln_g, ln_b)


def _rope_tables(n_tok):
    rows = n_tok // GRID_W
    row = np.broadcast_to(np.arange(rows, dtype=np.float64)[:, None], (rows, GRID_W)).reshape(-1)
    col = np.broadcast_to(np.arange(GRID_W, dtype=np.float64)[None, :], (rows, GRID_W)).reshape(-1)
    inv_freq = ROPE_THETA ** (-np.arange(0, ROPE_AXIS_DIM, 2, dtype=np.float64) / ROPE_AXIS_DIM)
    ang_r = row[:, None] * inv_freq[None, :]
    ang_c = col[:, None] * inv_freq[None, :]
    ang = np.concatenate([ang_r, ang_r, ang_c, ang_c], axis=-1)
    cos, sin = np.cos(ang).astype(np.float32), np.sin(ang).astype(np.float32)
    low = (np.arange(HEAD_DIM) % ROPE_AXIS_DIM) < ROPE_AXIS_DIM // 2
    sin_lo = np.where(low[None, :], -sin, np.float32(0.0))
    sin_hi = np.where(low[None, :], np.float32(0.0), sin)
    return jnp.asarray(cos), jnp.asarray(sin_lo), jnp.asarray(sin_hi)


def kernel(x, c, ctx, c_ctx, ada_w, ada_b, w_in, q_norm, k_norm, pool_w, pool_scale, w_out,
           ln1_g, ln1_b, router_w, w_gate, w_up, w_down, ln2_g, ln2_b):
    bsz, n_tok, d = x.shape
    n_ctx = ctx.shape[1]
    assert ada_w.shape[0] == DEPTH and d == D_MODEL and bsz + 1 <= MOD_ROWS
    cap = CAPACITY_FACTOR * n_tok // N_EXPERTS

    c8 = jnp.concatenate([c, c_ctx[None, :], jnp.zeros((MOD_ROWS - bsz - 1, d), F32)], axis=0)
    mod = _ada(c8, ada_w, ada_b)
    sh1, sc1, g1, sh2, sc2, g2 = [mod[:, k * d:(k + 1) * d] for k in range(6)]

    x2 = x.reshape(bsz * n_tok, d)
    cos, sin_lo, sin_hi = _rope_tables(n_tok)
    u, q, kt, v = _inproj(x2, sc1, sh1, w_in, q_norm, k_norm, cos, sin_lo, sin_hi, bsz, n_tok)
    kct, vc = _ctxkv(ctx.reshape(bsz * n_ctx, d), sc1, sh1, w_in, k_norm, bsz, n_ctx)

    pool = _pool(u, pool_w, pool_scale, bsz, n_tok)
    attn = _attn(q, kt, v, kct, vc, bsz, n_tok, n_ctx)

    rw_pad = jnp.pad(router_w[0], ((0, 0), (0, LANES - N_EXPERTS)))
    acc0, h2ext = _outproj(pool, attn, x2, w_out, g1, sc2, sh2, ln1_g, ln1_b, rw_pad, bsz, n_tok)

    idx = _topk(h2ext, bsz, n_tok, cap)
    idx_rows = (idx + (jnp.arange(bsz, dtype=jnp.int32) * n_tok)[:, None, None])
    idx_rows = idx_rows.transpose(1, 0, 2).reshape(N_EXPERTS, 1, bsz * cap)

    acc = _moe(idx_rows, h2ext, acc0, w_gate, w_up, w_down, g2, cap)
    out = _final(acc, ln2_g, ln2_b)
    return out.reshape(bsz, n_tok, d)
```

```python
import functools

import jax
import jax.numpy as jnp
import numpy as np
from jax import lax
from jax.experimental import pallas as pl
from jax.experimental.pallas import tpu as pltpu

F32 = jnp.float32
BF16 = jnp.bfloat16

D_MODEL = 2048
GRID_W = 64
HEAD_DIM = 128
ATTN_WIDTH = D_MODEL // 2
N_HEADS = ATTN_WIDTH // HEAD_DIM
N_KV_HEADS = 2
Q_PER_KV = N_HEADS // N_KV_HEADS
KV_WIDTH = N_KV_HEADS * HEAD_DIM
POOL_WIDTH = D_MODEL - ATTN_WIDTH
POOL_WINDOWS = (2, 4, 8, 16)
POOL_GROUP = POOL_WIDTH // len(POOL_WINDOWS)
Q_OFF = POOL_WIDTH
K_OFF = Q_OFF + ATTN_WIDTH
V_OFF = K_OFF + KV_WIDTH
IN_WIDTH = V_OFF + KV_WIDTH
ROPE_AXIS_DIM = HEAD_DIM // 2
ROPE_THETA = 10000.0
ATTN_SCALE = HEAD_DIM ** -0.5
Q_SCALE = ATTN_SCALE * float(np.log2(np.e))
V_EXT = 2 * HEAD_DIM
N_EXPERTS = 16
CAPACITY_FACTOR = 2
LN_EPS = 1e-6
QK_EPS = 1e-6
DEPTH = 1
ALPHA = (2.0 * DEPTH) ** 0.25

LANES = 128
SUBLANES = 8
V7X_VMEM_BYTES = 64 * 1024 * 1024
MOD_ROWS = SUBLANES
HALO = SUBLANES

ADA_TN = 1024
SUB_ROWS = 256
INPROJ_TM = 512
ATTN_TQ = 512
ATTN_SUBTILES = 2
ATTN_TK = 512
OUTPROJ_TM = 512
MOE_TF = 512
MOE_XGATHER_STEPS = 8
MOE_SCATTER_STEPS = 4
FINAL_TM = 512
PREFIX_BLK = 256
COUNT_CHAINS = 8


def _vmem_limit(nbytes):
    return int(min(nbytes, V7X_VMEM_BYTES - 4 * 1024 * 1024))


def _cparams(sem, nbytes):
    return pltpu.CompilerParams(dimension_semantics=sem, vmem_limit_bytes=_vmem_limit(nbytes))


def _sigmoid(x):
    return 1.0 / (1.0 + jnp.exp(-x))


def _ada_kernel(c_ref, w_ref, b_ref, o_ref):
    cv = c_ref[...]
    s = cv * _sigmoid(cv)
    o_ref[...] = jnp.dot(s, w_ref[0], preferred_element_type=F32) + b_ref[...]


def _ada(c8, ada_w, ada_b):
    d = c8.shape[1]
    n_out = ada_w.shape[2]
    return pl.pallas_call(
        _ada_kernel,
        grid=(n_out // ADA_TN,),
        in_specs=[
            pl.BlockSpec((MOD_ROWS, d), lambda j: (0, 0)),
            pl.BlockSpec((1, d, ADA_TN), lambda j: (0, 0, j)),
            pl.BlockSpec((1, ADA_TN), lambda j: (0, j)),
        ],
        out_specs=pl.BlockSpec((MOD_ROWS, ADA_TN), lambda j: (0, j)),
        out_shape=jax.ShapeDtypeStruct((MOD_ROWS, n_out), F32),
        compiler_params=_cparams(("arbitrary",), 2 * d * ADA_TN * 4 + (8 << 20)),
        name="ada",
    )(c8, ada_w, ada_b)


def _ones_column(n):
    lane = lax.broadcasted_iota(jnp.int32, (n, V_EXT - HEAD_DIM), 1)
    return jnp.where(lane == 0, 1.0, 0.0).astype(BF16)


def _rms(xh, g):
    ms = jnp.mean(xh * xh, axis=-1, keepdims=True)
    return xh * lax.rsqrt(ms + QK_EPS) * g


def _rope(xn, cos, sin_lo, sin_hi):
    return (xn * cos + pltpu.roll(xn, HEAD_DIM - ROPE_AXIS_DIM // 2, 1) * sin_lo
            + pltpu.roll(xn, ROPE_AXIS_DIM // 2, 1) * sin_hi)


def _inproj_kernel(x_ref, sc_ref, sh_ref, w_ref, qn_ref, kn_ref, cos_ref, slo_ref, shi_ref,
                   u_ref, q_ref, kt_ref, v_ref, *, tiles_per_batch, sub):
    b = pl.program_id(0) // tiles_per_batch
    sc = 1.0 + sc_ref[pl.ds(b, 1), :]
    sh = sh_ref[pl.ds(b, 1), :]
    qn = qn_ref[...]
    kn = kn_ref[...]
    for r0 in range(0, x_ref.shape[0], sub):
        rows = slice(r0, r0 + sub)
        h = x_ref[rows, :] * sc + sh
        proj = jnp.dot(h, w_ref[0], preferred_element_type=F32)
        u_ref[rows, :] = proj[:, :Q_OFF]
        cos = cos_ref[rows, :]
        slo = slo_ref[rows, :]
        shi = shi_ref[rows, :]
        for hd in range(N_HEADS):
            xh = proj[:, Q_OFF + hd * HEAD_DIM:Q_OFF + (hd + 1) * HEAD_DIM]
            qr = _rope(_rms(xh, qn), cos, slo, shi) * Q_SCALE
            q_ref[0, hd, rows, :] = qr.astype(BF16)
        for hd in range(N_KV_HEADS):
            kh = proj[:, K_OFF + hd * HEAD_DIM:K_OFF + (hd + 1) * HEAD_DIM]
            kr = _rope(_rms(kh, kn), cos, slo, shi)
            kt_ref[0, hd, :, rows] = kr.T.astype(BF16)
            v_ref[0, hd, rows, :HEAD_DIM] = proj[:, V_OFF + hd * HEAD_DIM:V_OFF + (hd + 1) * HEAD_DIM].astype(BF16)
            v_ref[0, hd, rows, HEAD_DIM:] = _ones_column(sub)


def _inproj(x2, sc1, sh1, w_in, q_norm, k_norm, cos, sin_lo, sin_hi, bsz, n_tok):
    tm = INPROJ_TM
    tpb = n_tok // tm
    d = x2.shape[1]
    est = (2 * tm * d * 4 + d * IN_WIDTH * 4 + 2 * tm * IN_WIDTH * 4 + 2 * tm * IN_WIDTH * 4
           + (8 << 20))
    return pl.pallas_call(
        functools.partial(_inproj_kernel, tiles_per_batch=tpb, sub=SUB_ROWS),
        grid=(bsz * tpb,),
        in_specs=[
            pl.BlockSpec((tm, d), lambda i: (i, 0)),
            pl.BlockSpec((MOD_ROWS, d), lambda i: (0, 0)),
            pl.BlockSpec((MOD_ROWS, d), lambda i: (0, 0)),
            pl.BlockSpec((1, d, IN_WIDTH), lambda i: (0, 0, 0)),
            pl.BlockSpec((1, HEAD_DIM), lambda i: (0, 0)),
            pl.BlockSpec((1, HEAD_DIM), lambda i: (0, 0)),
            pl.BlockSpec((tm, HEAD_DIM), lambda i: (i % tpb, 0)),
            pl.BlockSpec((tm, HEAD_DIM), lambda i: (i % tpb, 0)),
            pl.BlockSpec((tm, HEAD_DIM), lambda i: (i % tpb, 0)),
        ],
        out_specs=[
            pl.BlockSpec((tm, POOL_WIDTH), lambda i: (i, 0)),
            pl.BlockSpec((1, N_HEADS, tm, HEAD_DIM), lambda i: (i // tpb, 0, i % tpb, 0)),
            pl.BlockSpec((1, N_KV_HEADS, HEAD_DIM, tm), lambda i: (i // tpb, 0, 0, i % tpb)),
            pl.BlockSpec((1, N_KV_HEADS, tm, V_EXT), lambda i: (i // tpb, 0, i % tpb, 0)),
        ],
        out_shape=[
            jax.ShapeDtypeStruct((bsz * n_tok, POOL_WIDTH), F32),
            jax.ShapeDtypeStruct((bsz, N_HEADS, n_tok, HEAD_DIM), BF16),
            jax.ShapeDtypeStruct((bsz, N_KV_HEADS, HEAD_DIM, n_tok), BF16),
            jax.ShapeDtypeStruct((bsz, N_KV_HEADS, n_tok, V_EXT), BF16),
        ],
        compiler_params=_cparams(("arbitrary",), est),
        name="inproj",
    )(x2, sc1, sh1, w_in, q_norm, k_norm, cos, sin_lo, sin_hi)


def _ctxkv_kernel(ctx_ref, sc_ref, sh_ref, w_ref, kn_ref, kt_ref, v_ref, *, bsz, n_ctx):
    hc = ctx_ref[...] * (1.0 + sc_ref[2:3, :]) + sh_ref[2:3, :]
    kv = jnp.dot(hc, w_ref[0], preferred_element_type=F32)
    kn = kn_ref[...]
    for b in range(bsz):
        rows = slice(b * n_ctx, (b + 1) * n_ctx)
        for hd in range(N_KV_HEADS):
            kh = kv[rows, hd * HEAD_DIM:(hd + 1) * HEAD_DIM]
            kt_ref[b, hd] = _rms(kh, kn).T.astype(BF16)
            v_ref[b, hd, :, :HEAD_DIM] = kv[rows, KV_WIDTH + hd * HEAD_DIM:KV_WIDTH + (hd + 1) * HEAD_DIM].astype(BF16)
            v_ref[b, hd, :, HEAD_DIM:] = _ones_column(n_ctx)


def _ctxkv(ctx2, sc1, sh1, w_in, k_norm, bsz, n_ctx):
    d = ctx2.shape[1]
    kvw = 2 * KV_WIDTH
    return pl.pallas_call(
        functools.partial(_ctxkv_kernel, bsz=bsz, n_ctx=n_ctx),
        grid=(1,),
        in_specs=[
            pl.BlockSpec((bsz * n_ctx, d), lambda i: (0, 0)),
            pl.BlockSpec((MOD_ROWS, d), lambda i: (0, 0)),
            pl.BlockSpec((MOD_ROWS, d), lambda i: (0, 0)),
            pl.BlockSpec((1, d, kvw), lambda i: (0, 0, K_OFF // kvw)),
            pl.BlockSpec((1, HEAD_DIM), lambda i: (0, 0)),
        ],
        out_specs=[
            pl.BlockSpec((bsz, N_KV_HEADS, HEAD_DIM, n_ctx), lambda i: (0, 0, 0, 0)),
            pl.BlockSpec((bsz, N_KV_HEADS, n_ctx, V_EXT), lambda i: (0, 0, 0, 0)),
        ],
        out_shape=[
            jax.ShapeDtypeStruct((bsz, N_KV_HEADS, HEAD_DIM, n_ctx), BF16),
            jax.ShapeDtypeStruct((bsz, N_KV_HEADS, n_ctx, V_EXT), BF16),
        ],
        compiler_params=_cparams(("arbitrary",), 2 * (bsz * n_ctx * d * 4 + d * kvw * 4) + (8 << 20)),
        name="ctxkv",
    )(ctx2, sc1, sh1, w_in, k_norm)


def _attn_kernel(q_ref, kt_ref, v_ref, kct_ref, vc_ref, o_ref, *, tq, tk, n_tok, n_sub):
    ts = tq // n_sub
    rows = Q_PER_KV * ts
    qs = [q_ref[0, :, i * ts:(i + 1) * ts, :].reshape(rows, HEAD_DIM) for i in range(n_sub)]

    def step(q, carry, kt_c, v_c):
        m, acc = carry
        s = jnp.dot(q, kt_c, preferred_element_type=F32)
        m_new = jnp.maximum(m, jnp.max(s, axis=-1, keepdims=True))
        alpha = jnp.exp2(m - m_new)
        p = jnp.exp2(s - m_new)
        acc = alpha * acc + jnp.dot(p.astype(BF16), v_c, preferred_element_type=F32)
        return m_new, acc

    carries = [(jnp.full((rows, 1), -jnp.inf, F32), jnp.zeros((rows, V_EXT), F32)) for _ in range(n_sub)]
    carries = [step(q, c, kct_ref[0, 0], vc_ref[0, 0]) for q, c in zip(qs, carries)]
    for j in range(n_tok // tk):
        carries = [step(q, c, kt_ref[0, 0, :, j * tk:(j + 1) * tk], v_ref[0, 0, j * tk:(j + 1) * tk, :])
                   for q, c in zip(qs, carries)]
    for i, (_, acc) in enumerate(carries):
        out = acc[:, :HEAD_DIM] / acc[:, HEAD_DIM:HEAD_DIM + 1]
        for g in range(Q_PER_KV):
            o_ref[i * ts:(i + 1) * ts, g * HEAD_DIM:(g + 1) * HEAD_DIM] = out[g * ts:(g + 1) * ts].astype(BF16)


def _attn(q, kt, v, kct, vc, bsz, n_tok, n_ctx):
    tq, tk = ATTN_TQ, ATTN_TK
    nq = n_tok // tq
    gw = Q_PER_KV * HEAD_DIM
    return pl.pallas_call(
        functools.partial(_attn_kernel, tq=tq, tk=tk, n_tok=n_tok, n_sub=ATTN_SUBTILES),
        grid=(bsz, N_KV_HEADS, nq),
        in_specs=[
            pl.BlockSpec((1, Q_PER_KV, tq, HEAD_DIM), lambda b, h, i: (b, h, i, 0)),
            pl.BlockSpec((1, 1, HEAD_DIM, n_tok), lambda b, h, i: (b, h, 0, 0)),
            pl.BlockSpec((1, 1, n_tok, V_EXT), lambda b, h, i: (b, h, 0, 0)),
            pl.BlockSpec((1, 1, HEAD_DIM, n_ctx), lambda b, h, i: (b, h, 0, 0)),
            pl.BlockSpec((1, 1, n_ctx, V_EXT), lambda b, h, i: (b, h, 0, 0)),
        ],
        out_specs=pl.BlockSpec((tq, gw), lambda b, h, i: (b * nq + i, h)),
        out_shape=jax.ShapeDtypeStruct((bsz * n_tok, ATTN_WIDTH), BF16),
        compiler_params=_cparams(("arbitrary", "arbitrary", "arbitrary"), 40 << 20),
        name="attn",
    )(q, kt, v, kct, vc)


def _layer_norm(r, g, b):
    mu = jnp.mean(r, axis=-1, keepdims=True)
    rc = r - mu
    var = jnp.mean(rc * rc, axis=-1, keepdims=True)
    return rc * lax.rsqrt(var + LN_EPS) * g + b


def _pool_mix(ext_ref, uc_ref, pw_ref, ps_ref, t0, r0, sub, n_tok, gi):
    half = POOL_WINDOWS[gi] // 2
    cols = slice(gi * POOL_GROUP, (gi + 1) * POOL_GROUP)
    acc = ext_ref[HALO + r0 - half:HALO + r0 - half + sub, cols]
    for dd in range(-half + 1, half):
        acc = acc + ext_ref[HALO + r0 + dd:HALO + r0 + dd + sub, cols]
    t = t0 + r0 + lax.broadcasted_iota(jnp.int32, (sub, POOL_GROUP), 0)
    cnt = (jnp.minimum(t + half, n_tok) - jnp.maximum(t - half, 0)).astype(F32)
    diff = acc / cnt - uc_ref[r0:r0 + sub, cols]
    return jnp.dot(diff, pw_ref[0, gi], preferred_element_type=F32) * ps_ref[:, cols]


def _outproj_kernel(up_ref, uc_ref, un_ref, pw_ref, ps_ref, attn_ref, x_ref, w_ref, g1_ref, sc_ref, sh_ref,
                    lg_ref, lb_ref, rw_ref, acc_ref, h_ref, ext_ref, *, tiles_per_batch, sub, n_tok):
    b = pl.program_id(0) // tiles_per_batch
    it = pl.program_id(0) % tiles_per_batch
    tm, d = x_ref.shape
    ext_ref[0:HALO, :] = jnp.where(it == 0, 0.0, up_ref[...])
    ext_ref[HALO:HALO + tm, :] = uc_ref[...]
    ext_ref[HALO + tm:HALO + tm + HALO, :] = jnp.where(it == tiles_per_batch - 1, 0.0, un_ref[...])
    g1 = g1_ref[pl.ds(b, 1), :]
    sc = 1.0 + sc_ref[pl.ds(b, 1), :]
    sh = sh_ref[pl.ds(b, 1), :]
    rw = rw_ref[...].astype(BF16)
    for r0 in range(0, tm, sub):
        rows = slice(r0, r0 + sub)
        y = jnp.dot(attn_ref[rows, :].astype(F32), w_ref[0, POOL_WIDTH:, :], preferred_element_type=F32)
        for gi in range(len(POOL_WINDOWS)):
            mixed = _pool_mix(ext_ref, uc_ref, pw_ref, ps_ref, it * tm, r0, sub, n_tok, gi)
            y = y + jnp.dot(mixed, w_ref[0, gi * POOL_GROUP:(gi + 1) * POOL_GROUP, :], preferred_element_type=F32)
        r = ALPHA * x_ref[rows, :] + g1 * y
        xm = _layer_norm(r, lg_ref[...], lb_ref[...])
        acc_ref[rows, :] = ALPHA * xm
        h2 = xm * sc + sh
        h_ref[rows, :d] = h2
        logits = jnp.dot(h2.astype(BF16), rw, preferred_element_type=F32)
        valid = lax.broadcasted_iota(jnp.int32, logits.shape, 1) < N_EXPERTS
        lg = jnp.where(valid, logits, -jnp.inf)
        ex = jnp.exp(lg - jnp.max(lg, axis=-1, keepdims=True))
        h_ref[rows, d:] = ex / jnp.sum(ex, axis=-1, keepdims=True)


def _outproj(u, pool_w, pool_scale, attn, x2, w_out, g1, sc2, sh2, ln_g, ln_b, rw_pad, bsz, n_tok):
    tm = OUTPROJ_TM
    tpb = n_tok // tm
    d = x2.shape[1]
    n_rows = bsz * n_tok
    hb = tm // HALO
    last_hblk = n_rows // HALO - 1
    est = d * d * 4 + 2 * tm * d * 4 * 4 + 8 * tm * d * 4 + (8 << 20)
    return pl.pallas_call(
        functools.partial(_outproj_kernel, tiles_per_batch=tpb, sub=SUB_ROWS, n_tok=n_tok),
        grid=(bsz * tpb,),
        in_specs=[
            pl.BlockSpec((HALO, POOL_WIDTH), lambda i: (jnp.maximum(i * hb - 1, 0), 0)),
            pl.BlockSpec((tm, POOL_WIDTH), lambda i: (i, 0)),
            pl.BlockSpec((HALO, POOL_WIDTH), lambda i: (jnp.minimum((i + 1) * hb, last_hblk), 0)),
            pl.BlockSpec((1, len(POOL_WINDOWS), POOL_GROUP, POOL_GROUP), lambda i: (0, 0, 0, 0)),
            pl.BlockSpec((1, POOL_WIDTH), lambda i: (0, 0)),
            pl.BlockSpec((tm, ATTN_WIDTH), lambda i: (i, 0)),
            pl.BlockSpec((tm, d), lambda i: (i, 0)),
            pl.BlockSpec((1, d, d), lambda i: (0, 0, 0)),
            pl.BlockSpec((MOD_ROWS, d), lambda i: (0, 0)),
            pl.BlockSpec((MOD_ROWS, d), lambda i: (0, 0)),
            pl.BlockSpec((MOD_ROWS, d), lambda i: (0, 0)),
            pl.BlockSpec((1, d), lambda i: (0, 0)),
            pl.BlockSpec((1, d), lambda i: (0, 0)),
            pl.BlockSpec((d, LANES), lambda i: (0, 0)),
        ],
        out_specs=[
            pl.BlockSpec((tm, d), lambda i: (i, 0)),
            pl.BlockSpec((tm, d + LANES), lambda i: (i, 0)),
        ],
        out_shape=[
            jax.ShapeDtypeStruct((n_rows, d), F32),
            jax.ShapeDtypeStruct((n_rows, d + LANES), F32),
        ],
        scratch_shapes=[pltpu.VMEM((tm + 2 * HALO, POOL_WIDTH), F32)],
        compiler_params=_cparams(("arbitrary",), est),
        name="outproj",
    )(u, u, u, pool_w, pool_scale, attn, x2, w_out, g1, sc2, sh2, ln_g, ln_b, rw_pad)


def _prefix_incl(flags_bf16, out_ref, n_tok):
    blk = PREFIX_BLK
    tri = (lax.broadcasted_iota(jnp.int32, (blk, blk), 0)
           >= lax.broadcasted_iota(jnp.int32, (blk, blk), 1)).astype(BF16)
    carry = jnp.zeros((1, LANES), F32)
    for k in range(n_tok // blk):
        p = jnp.dot(tri, flags_bf16[k * blk:(k + 1) * blk], preferred_element_type=F32) + carry
        out_ref[k * blk:(k + 1) * blk, :] = p
        carry = p[blk - 1:blk, :]


def _topk_kernel(aff_ref, idx_ref, cnt_ref, part_ref, *, n_tok, cap):
    lane = lax.broadcasted_iota(jnp.int32, (1, LANES), 1)
    bits = jnp.where(lane < N_EXPERTS, pltpu.bitcast(aff_ref[...], jnp.int32), -1)

    seg = n_tok // COUNT_CHAINS

    def search(_, carry):
        lo, hi = carry
        mid = lo + ((hi - lo + 1) >> 1)
        n_ge = sum(jnp.sum((bits[k * seg:(k + 1) * seg] >= mid).astype(jnp.int32), axis=0, keepdims=True)
                   for k in range(COUNT_CHAINS))
        ok = n_ge >= cap
        return jnp.where(ok, mid, lo), jnp.where(ok, hi, mid - 1)

    lo0 = jnp.zeros((1, LANES), jnp.int32)
    hi0 = jnp.full((1, LANES), 0x7F800000, jnp.int32)
    thr, _ = lax.fori_loop(0, 32, search, (lo0, hi0))

    gt = bits > thr
    eq = bits == thr
    need = (cap - jnp.sum(gt.astype(jnp.int32), axis=0, keepdims=True)).astype(F32)
    _prefix_incl(jnp.where(eq, 1.0, 0.0).astype(BF16), cnt_ref, n_tok)
    sel = gt | (eq & (cnt_ref[...] <= need))
    _prefix_incl(jnp.where(sel, 1.0, 0.0).astype(BF16), cnt_ref, n_tok)

    slot = lax.broadcasted_iota(jnp.int32, (1, cap), 1).astype(F32)
    blk = PREFIX_BLK
    part_ref[...] = jnp.zeros_like(part_ref)

    def count(k, _):
        r0 = pl.multiple_of(k * blk, blk)
        cblk = cnt_ref[pl.ds(r0, blk), :]
        for e in range(N_EXPERTS):
            hit = (cblk[:, e:e + 1] <= slot).astype(F32)
            part_ref[e] += jnp.sum(hit.reshape(blk // SUBLANES, SUBLANES, cap), axis=0)
        return 0

    lax.fori_loop(0, n_tok // blk, count, 0)
    for e in range(N_EXPERTS):
        idx_ref[0, e:e + 1, :] = jnp.sum(part_ref[e], axis=0, keepdims=True).astype(jnp.int32)


def _topk(h2ext, bsz, n_tok, cap):
    aff_blk = h2ext.shape[1] // LANES - 1
    return pl.pallas_call(
        functools.partial(_topk_kernel, n_tok=n_tok, cap=cap),
        grid=(bsz,),
        in_specs=[pl.BlockSpec((n_tok, LANES), lambda b: (b, aff_blk))],
        out_specs=pl.BlockSpec((1, N_EXPERTS, cap), lambda b: (b, 0, 0)),
        out_shape=jax.ShapeDtypeStruct((bsz, N_EXPERTS, cap), jnp.int32),
        scratch_shapes=[pltpu.VMEM((n_tok, LANES), F32), pltpu.VMEM((N_EXPERTS, SUBLANES, cap), F32)],
        compiler_params=_cparams(("arbitrary",), 32 << 20),
        name="topk",
    )(h2ext)


def _moe_kernel(idxp_ref, idxc_ref, idxn_ref, h_hbm, acc_in_hbm, wg_ref, wu_ref, wd_ref, g2_ref, acc_hbm,
                xe_ref, xb_ref, gate_ref, y_ref, ab_ref, sem, *, n_f, n_rows, d, cap, n_x, n_sc):
    del acc_in_hbm
    e = pl.program_id(0)
    f = pl.program_id(1)
    n_e = pl.num_programs(0)
    sem_x, sem_a, sem_s = sem.at[0], sem.at[1], sem.at[2]
    rx = n_rows // n_x
    rs = n_rows // n_sc

    def x_copy(idx_ref, g0, n, k):
        return pltpu.make_async_copy(h_hbm.at[pl.ds(idx_ref[0, 0, g0 + k], 1), :],
                                     xe_ref.at[pl.ds(g0, n), :].at[pl.ds(k, 1), :], sem_x)

    def a_copy(idx_ref, g0, n, k):
        return pltpu.make_async_copy(acc_hbm.at[pl.ds(idx_ref[0, 0, g0 + k], 1), :],
                                     ab_ref.at[pl.ds(g0, n), :].at[pl.ds(k, 1), :], sem_a)

    def s_copy(idx_ref, g0, n, k):
        return pltpu.make_async_copy(ab_ref.at[pl.ds(g0, n), :].at[pl.ds(k, 1), :],
                                     acc_hbm.at[pl.ds(idx_ref[0, 0, g0 + k], 1), :], sem_s)

    def start_window(make, idx_ref, j0, n):
        g0 = pl.multiple_of(j0, SUBLANES)
        for k in range(n):
            make(idx_ref, g0, n, k).start()

    def start_all(make, idx_ref):
        def body(g, _):
            start_window(make, idx_ref, g * SUBLANES, SUBLANES)
            return 0

        lax.fori_loop(0, n_rows // SUBLANES, body, 0)

    def wait_x():
        pltpu.make_async_copy(h_hbm.at[pl.ds(0, n_rows), :], xe_ref, sem_x).wait()

    def wait_a():
        pltpu.make_async_copy(acc_hbm.at[pl.ds(0, n_rows), :], ab_ref, sem_a).wait()

    def wait_s():
        pltpu.make_async_copy(ab_ref, acc_hbm.at[pl.ds(0, n_rows), :], sem_s).wait()

    @pl.when((e == 0) & (f == 0))
    def _():
        start_all(x_copy, idxc_ref)
        start_all(a_copy, idxc_ref)
        wait_a()

    @pl.when(f == 0)
    def _():
        wait_x()
        xb_ref[...] = xe_ref[:, :d].astype(BF16)
        lane = lax.broadcasted_iota(jnp.int32, (n_rows, LANES), 1)
        gate_ref[...] = jnp.sum(jnp.where(lane == e, xe_ref[:, d:], 0.0), axis=-1, keepdims=True)
        y_ref[...] = jnp.zeros_like(y_ref)

    @pl.when(f == n_sc + 1)
    def _():
        wait_s()

    def compute():
        x = xb_ref[...]
        a = jnp.dot(x, wg_ref[0, 0].astype(BF16), preferred_element_type=F32)
        b = jnp.dot(x, wu_ref[0, 0].astype(BF16), preferred_element_type=F32)
        hm = (a * _sigmoid(a) * b).astype(BF16)
        y_ref[...] += jnp.dot(hm, wd_ref[0, 0].astype(BF16), preferred_element_type=F32)

    def step_plain():
        compute()

    def step_scatter():
        start_window(x_copy, idxn_ref, (f - 1) * rx, rx)
        start_window(s_copy, idxp_ref, (f - 1) * rs, rs)
        compute()

    def step_gather():
        start_window(x_copy, idxn_ref, (f - 1) * rx, rx)
        start_window(a_copy, idxc_ref, (f - n_sc - 1) * rs, rs)
        compute()

    phase = jnp.where((f >= 1) & (f <= n_sc), 1, jnp.where((f > n_sc) & (f <= 2 * n_sc), 2, 0))
    lax.switch(phase, [step_plain, step_scatter, step_gather])

    @pl.when(f == n_f - 1)
    def _():
        wait_a()
        for bi in range(n_rows // cap):
            rows = slice(bi * cap, (bi + 1) * cap)
            ab_ref[rows, :] += (gate_ref[rows, :] * g2_ref[bi:bi + 1, :]) * y_ref[rows, :]

        @pl.when(e == n_e - 1)
        def _():
            wait_x()
            start_all(s_copy, idxc_ref)
            wait_s()


def _moe(idx_rows, h2ext, acc0, w_gate, w_up, w_down, g2, cap):
    d = acc0.shape[1]
    n_rows = idx_rows.shape[2]
    ff = w_gate.shape[3]
    tf = MOE_TF
    n_f = ff // tf
    n_x, n_sc = MOE_XGATHER_STEPS, MOE_SCATTER_STEPS
    assert n_x == 2 * n_sc and n_x + 1 <= n_f - 1 and n_rows % (n_x * SUBLANES) == 0
    est = (2 * 3 * d * tf * 4 + n_rows * h2ext.shape[1] * 4 + n_rows * d * 2 + n_rows * LANES * 4
           + 2 * n_rows * d * 4 + 3 * d * tf * 2 + 4 * n_rows * tf * 4 + (6 << 20))
    last = N_EXPERTS - 1
    return pl.pallas_call(
        functools.partial(_moe_kernel, n_f=n_f, n_rows=n_rows, d=d, cap=cap, n_x=n_x, n_sc=n_sc),
        grid=(N_EXPERTS, n_f),
        in_specs=[
            pl.BlockSpec((1, 1, n_rows), lambda e, f: (jnp.maximum(e - 1, 0), 0, 0), memory_space=pltpu.SMEM),
            pl.BlockSpec((1, 1, n_rows), lambda e, f: (e, 0, 0), memory_space=pltpu.SMEM),
            pl.BlockSpec((1, 1, n_rows), lambda e, f: (jnp.minimum(e + 1, last), 0, 0), memory_space=pltpu.SMEM),
            pl.BlockSpec(memory_space=pl.ANY),
            pl.BlockSpec(memory_space=pl.ANY),
            pl.BlockSpec((1, 1, d, tf), lambda e, f: (0, e, 0, f)),
            pl.BlockSpec((1, 1, d, tf), lambda e, f: (0, e, 0, f)),
            pl.BlockSpec((1, 1, tf, d), lambda e, f: (0, e, f, 0)),
            pl.BlockSpec((MOD_ROWS, d), lambda e, f: (0, 0)),
        ],
        out_specs=pl.BlockSpec(memory_space=pl.ANY),
        out_shape=jax.ShapeDtypeStruct(acc0.shape, F32),
        scratch_shapes=[
            pltpu.VMEM((n_rows, h2ext.shape[1]), F32),
            pltpu.VMEM((n_rows, d), BF16),
            pltpu.VMEM((n_rows, 1), F32),
            pltpu.VMEM((n_rows, d), F32),
            pltpu.VMEM((n_rows, d), F32),
            pltpu.SemaphoreType.DMA((3,)),
        ],
        input_output_aliases={4: 0},
        compiler_params=_cparams(("arbitrary", "arbitrary"), est),
        name="moe",
    )(idx_rows, idx_rows, idx_rows, h2ext, acc0, w_gate, w_up, w_down, g2)


def _final_kernel(a_ref, g_ref, b_ref, o_ref):
    o_ref[...] = _layer_norm(a_ref[...], g_ref[...], b_ref[...])


def _final(acc, ln_g, ln_b):
    n_rows, d = acc.shape
    tm = FINAL_TM
    return pl.pallas_call(
        _final_kernel,
        grid=(n_rows // tm,),
        in_specs=[
            pl.BlockSpec((tm, d), lambda i: (i, 0)),
            pl.BlockSpec((1, d), lambda i: (0, 0)),
            pl.BlockSpec((1, d), lambda i: (0, 0)),
        ],
        out_specs=pl.BlockSpec((tm, d), lambda i: (i, 0)),
        out_shape=jax.ShapeDtypeStruct((n_rows, d), F32),
        compiler_params=_cparams(("arbitrary",), 8 * tm * d * 4 + (8 << 20)),
        name="final_ln",
    )(acc, ln_g, ln_b)


def _rope_tables(n_tok):
    rows = n_tok // GRID_W
    row = np.broadcast_to(np.arange(rows, dtype=np.float64)[:, None], (rows, GRID_W)).reshape(-1)
    col = np.broadcast_to(np.arange(GRID_W, dtype=np.float64)[None, :], (rows, GRID_W)).reshape(-1)
    inv_freq = ROPE_THETA ** (-np.arange(0, ROPE_AXIS_DIM, 2, dtype=np.float64) / ROPE_AXIS_DIM)
    ang_r = row[:, None] * inv_freq[None, :]
    ang_c = col[:, None] * inv_freq[None, :]
    ang = np.concatenate([ang_r, ang_r, ang_c, ang_c], axis=-1)
    cos, sin = np.cos(ang).astype(np.float32), np.sin(ang).astype(np.float32)
    low = (np.arange(HEAD_DIM) % ROPE_AXIS_DIM) < ROPE_AXIS_DIM // 2
    sin_lo = np.where(low[None, :], -sin, np.float32(0.0))
    sin_hi = np.where(low[None, :], np.float32(0.0), sin)
    return jnp.asarray(cos), jnp.asarray(sin_lo), jnp.asarray(sin_hi)


def kernel(x, c, ctx, c_ctx, ada_w, ada_b, w_in, q_norm, k_norm, pool_w, pool_scale, w_out,
           ln1_g, ln1_b, router_w, w_gate, w_up, w_down, ln2_g, ln2_b):
    bsz, n_tok, d = x.shape
    n_ctx = ctx.shape[1]
    assert ada_w.shape[0] == DEPTH and d == D_MODEL and bsz + 1 <= MOD_ROWS
    cap = CAPACITY_FACTOR * n_tok // N_EXPERTS

    c8 = jnp.concatenate([c, c_ctx[None, :], jnp.zeros((MOD_ROWS - bsz - 1, d), F32)], axis=0)
    mod = _ada(c8, ada_w, ada_b)
    sh1, sc1, g1, sh2, sc2, g2 = [mod[:, k * d:(k + 1) * d] for k in range(6)]

    x2 = x.reshape(bsz * n_tok, d)
    cos, sin_lo, sin_hi = _rope_tables(n_tok)
    u, q, kt, v = _inproj(x2, sc1, sh1, w_in, q_norm, k_norm, cos, sin_lo, sin_hi, bsz, n_tok)
    kct, vc = _ctxkv(ctx.reshape(bsz * n_ctx, d), sc1, sh1, w_in, k_norm, bsz, n_ctx)

    attn = _attn(q, kt, v, kct, vc, bsz, n_tok, n_ctx)

    rw_pad = jnp.pad(router_w[0], ((0, 0), (0, LANES - N_EXPERTS)))
    acc0, h2ext = _outproj(u, pool_w, pool_scale, attn, x2, w_out, g1, sc2, sh2, ln1_g, ln1_b, rw_pad,
                           bsz, n_tok)

    idx = _topk(h2ext, bsz, n_tok, cap)
    idx_rows = (idx + (jnp.arange(bsz, dtype=jnp.int32) * n_tok)[:, None, None])
    idx_rows = idx_rows.transpose(1, 0, 2).reshape(N_EXPERTS, 1, bsz * cap)

    acc = _moe(idx_rows, h2ext, acc0, w_gate, w_up, w_down, g2, cap)
    out = _final(acc, ln2_g, ln2_b)
    return out.reshape(bsz, n_tok, d)
```

```python
import functools

import jax
import jax.numpy as jnp
import numpy as np
from jax import lax
from jax.experimental import pallas as pl
from jax.experimental.pallas import tpu as pltpu

F32 = jnp.float32
BF16 = jnp.bfloat16

D_MODEL = 2048
GRID_W = 64
HEAD_DIM = 128
ATTN_WIDTH = D_MODEL // 2
N_HEADS = ATTN_WIDTH // HEAD_DIM
N_KV_HEADS = 2
Q_PER_KV = N_HEADS // N_KV_HEADS
KV_WIDTH = N_KV_HEADS * HEAD_DIM
POOL_WIDTH = D_MODEL - ATTN_WIDTH
POOL_WINDOWS = (2, 4, 8, 16)
POOL_GROUP = POOL_WIDTH // len(POOL_WINDOWS)
Q_OFF = POOL_WIDTH
K_OFF = Q_OFF + ATTN_WIDTH
V_OFF = K_OFF + KV_WIDTH
IN_WIDTH = V_OFF + KV_WIDTH
ROPE_AXIS_DIM = HEAD_DIM // 2
ROPE_THETA = 10000.0
ATTN_SCALE = HEAD_DIM ** -0.5
Q_SCALE = ATTN_SCALE * float(np.log2(np.e))
V_EXT = 2 * HEAD_DIM
N_EXPERTS = 16
CAPACITY_FACTOR = 2
LN_EPS = 1e-6
QK_EPS = 1e-6
DEPTH = 1
ALPHA = (2.0 * DEPTH) ** 0.25

LANES = 128
SUBLANES = 8
V7X_VMEM_BYTES = 64 * 1024 * 1024
MOD_ROWS = SUBLANES
HALO = SUBLANES

ADA_TN = 1024
SUB_ROWS = 256
INPROJ_TM = 512
ATTN_TQ = 1024
ATTN_SUBTILES = 4
ATTN_TK = 512
OUTPROJ_TM = 512
MOE_TF = 512
MOE_XGATHER_STEPS = 8
MOE_SCATTER_STEPS = 4
FINAL_TM = 512
PREFIX_BLK = 256
COUNT_CHAINS = 8


def _vmem_limit(nbytes):
    return int(min(nbytes, V7X_VMEM_BYTES - 4 * 1024 * 1024))


def _cparams(sem, nbytes):
    return pltpu.CompilerParams(dimension_semantics=sem, vmem_limit_bytes=_vmem_limit(nbytes))


def _sigmoid(x):
    return 1.0 / (1.0 + jnp.exp(-x))


def _ada_kernel(c_ref, w_ref, b_ref, o_ref):
    cv = c_ref[...]
    s = cv * _sigmoid(cv)
    o_ref[...] = jnp.dot(s, w_ref[0], preferred_element_type=F32) + b_ref[...]


def _ada(c8, ada_w, ada_b):
    d = c8.shape[1]
    n_out = ada_w.shape[2]
    return pl.pallas_call(
        _ada_kernel,
        grid=(n_out // ADA_TN,),
        in_specs=[
            pl.BlockSpec((MOD_ROWS, d), lambda j: (0, 0)),
            pl.BlockSpec((1, d, ADA_TN), lambda j: (0, 0, j)),
            pl.BlockSpec((1, ADA_TN), lambda j: (0, j)),
        ],
        out_specs=pl.BlockSpec((MOD_ROWS, ADA_TN), lambda j: (0, j)),
        out_shape=jax.ShapeDtypeStruct((MOD_ROWS, n_out), F32),
        compiler_params=_cparams(("arbitrary",), 2 * d * ADA_TN * 4 + (8 << 20)),
        name="ada",
    )(c8, ada_w, ada_b)


def _ones_column(n):
    lane = lax.broadcasted_iota(jnp.int32, (n, V_EXT - HEAD_DIM), 1)
    return jnp.where(lane == 0, 1.0, 0.0).astype(BF16)


def _rms(xh, g):
    ms = jnp.mean(xh * xh, axis=-1, keepdims=True)
    return xh * lax.rsqrt(ms + QK_EPS) * g


def _rope(xn, cos, sin_lo, sin_hi):
    return (xn * cos + pltpu.roll(xn, HEAD_DIM - ROPE_AXIS_DIM // 2, 1) * sin_lo
            + pltpu.roll(xn, ROPE_AXIS_DIM // 2, 1) * sin_hi)


def _inproj_kernel(x_ref, sc_ref, sh_ref, w_ref, qn_ref, kn_ref, cos_ref, slo_ref, shi_ref,
                   u_ref, q_ref, kt_ref, v_ref, *, tiles_per_batch, sub):
    b = pl.program_id(0) // tiles_per_batch
    sc = 1.0 + sc_ref[pl.ds(b, 1), :]
    sh = sh_ref[pl.ds(b, 1), :]
    qn = qn_ref[...]
    kn = kn_ref[...]
    for r0 in range(0, x_ref.shape[0], sub):
        rows = slice(r0, r0 + sub)
        h = x_ref[rows, :] * sc + sh
        proj = jnp.dot(h, w_ref[0], preferred_element_type=F32)
        u_ref[rows, :] = proj[:, :Q_OFF]
        cos = cos_ref[rows, :]
        slo = slo_ref[rows, :]
        shi = shi_ref[rows, :]
        for hd in range(N_HEADS):
            xh = proj[:, Q_OFF + hd * HEAD_DIM:Q_OFF + (hd + 1) * HEAD_DIM]
            qr = _rope(_rms(xh, qn), cos, slo, shi) * Q_SCALE
            q_ref[0, hd, rows, :] = qr.astype(BF16)
        for hd in range(N_KV_HEADS):
            kh = proj[:, K_OFF + hd * HEAD_DIM:K_OFF + (hd + 1) * HEAD_DIM]
            kr = _rope(_rms(kh, kn), cos, slo, shi)
            kt_ref[0, hd, :, rows] = kr.T.astype(BF16)
            v_ref[0, hd, rows, :HEAD_DIM] = proj[:, V_OFF + hd * HEAD_DIM:V_OFF + (hd + 1) * HEAD_DIM].astype(BF16)
            v_ref[0, hd, rows, HEAD_DIM:] = _ones_column(sub)


def _inproj(x2, sc1, sh1, w_in, q_norm, k_norm, cos, sin_lo, sin_hi, bsz, n_tok):
    tm = INPROJ_TM
    tpb = n_tok // tm
    d = x2.shape[1]
    est = (2 * tm * d * 4 + d * IN_WIDTH * 4 + 2 * tm * IN_WIDTH * 4 + 2 * tm * IN_WIDTH * 4
           + (8 << 20))
    return pl.pallas_call(
        functools.partial(_inproj_kernel, tiles_per_batch=tpb, sub=SUB_ROWS),
        grid=(bsz * tpb,),
        in_specs=[
            pl.BlockSpec((tm, d), lambda i: (i, 0)),
            pl.BlockSpec((MOD_ROWS, d), lambda i: (0, 0)),
            pl.BlockSpec((MOD_ROWS, d), lambda i: (0, 0)),
            pl.BlockSpec((1, d, IN_WIDTH), lambda i: (0, 0, 0)),
            pl.BlockSpec((1, HEAD_DIM), lambda i: (0, 0)),
            pl.BlockSpec((1, HEAD_DIM), lambda i: (0, 0)),
            pl.BlockSpec((tm, HEAD_DIM), lambda i: (i % tpb, 0)),
            pl.BlockSpec((tm, HEAD_DIM), lambda i: (i % tpb, 0)),
            pl.BlockSpec((tm, HEAD_DIM), lambda i: (i % tpb, 0)),
        ],
        out_specs=[
            pl.BlockSpec((tm, POOL_WIDTH), lambda i: (i, 0)),
            pl.BlockSpec((1, N_HEADS, tm, HEAD_DIM), lambda i: (i // tpb, 0, i % tpb, 0)),
            pl.BlockSpec((1, N_KV_HEADS, HEAD_DIM, tm), lambda i: (i // tpb, 0, 0, i % tpb)),
            pl.BlockSpec((1, N_KV_HEADS, tm, V_EXT), lambda i: (i // tpb, 0, i % tpb, 0)),
        ],
        out_shape=[
            jax.ShapeDtypeStruct((bsz * n_tok, POOL_WIDTH), F32),
            jax.ShapeDtypeStruct((bsz, N_HEADS, n_tok, HEAD_DIM), BF16),
            jax.ShapeDtypeStruct((bsz, N_KV_HEADS, HEAD_DIM, n_tok), BF16),
            jax.ShapeDtypeStruct((bsz, N_KV_HEADS, n_tok, V_EXT), BF16),
        ],
        compiler_params=_cparams(("arbitrary",), est),
        name="inproj",
    )(x2, sc1, sh1, w_in, q_norm, k_norm, cos, sin_lo, sin_hi)


def _ctxkv_kernel(ctx_ref, sc_ref, sh_ref, w_ref, kn_ref, kt_ref, v_ref, *, bsz, n_ctx):
    hc = ctx_ref[...] * (1.0 + sc_ref[2:3, :]) + sh_ref[2:3, :]
    kv = jnp.dot(hc, w_ref[0], preferred_element_type=F32)
    kn = kn_ref[...]
    for b in range(bsz):
        rows = slice(b * n_ctx, (b + 1) * n_ctx)
        for hd in range(N_KV_HEADS):
            kh = kv[rows, hd * HEAD_DIM:(hd + 1) * HEAD_DIM]
            kt_ref[b, hd] = _rms(kh, kn).T.astype(BF16)
            v_ref[b, hd, :, :HEAD_DIM] = kv[rows, KV_WIDTH + hd * HEAD_DIM:KV_WIDTH + (hd + 1) * HEAD_DIM].astype(BF16)
            v_ref[b, hd, :, HEAD_DIM:] = _ones_column(n_ctx)


def _ctxkv(ctx2, sc1, sh1, w_in, k_norm, bsz, n_ctx):
    d = ctx2.shape[1]
    kvw = 2 * KV_WIDTH
    return pl.pallas_call(
        functools.partial(_ctxkv_kernel, bsz=bsz, n_ctx=n_ctx),
        grid=(1,),
        in_specs=[
            pl.BlockSpec((bsz * n_ctx, d), lambda i: (0, 0)),
            pl.BlockSpec((MOD_ROWS, d), lambda i: (0, 0)),
            pl.BlockSpec((MOD_ROWS, d), lambda i: (0, 0)),
            pl.BlockSpec((1, d, kvw), lambda i: (0, 0, K_OFF // kvw)),
            pl.BlockSpec((1, HEAD_DIM), lambda i: (0, 0)),
        ],
        out_specs=[
            pl.BlockSpec((bsz, N_KV_HEADS, HEAD_DIM, n_ctx), lambda i: (0, 0, 0, 0)),
            pl.BlockSpec((bsz, N_KV_HEADS, n_ctx, V_EXT), lambda i: (0, 0, 0, 0)),
        ],
        out_shape=[
            jax.ShapeDtypeStruct((bsz, N_KV_HEADS, HEAD_DIM, n_ctx), BF16),
            jax.ShapeDtypeStruct((bsz, N_KV_HEADS, n_ctx, V_EXT), BF16),
        ],
        compiler_params=_cparams(("arbitrary",), 2 * (bsz * n_ctx * d * 4 + d * kvw * 4) + (8 << 20)),
        name="ctxkv",
    )(ctx2, sc1, sh1, w_in, k_norm)


def _attn_kernel(q_ref, kt_ref, v_ref, kct_ref, vc_ref, o_ref, *, tq, tk, n_tok, n_sub):
    ts = tq // n_sub
    rows = Q_PER_KV * ts
    qs = [q_ref[0, :, i * ts:(i + 1) * ts, :].reshape(rows, HEAD_DIM) for i in range(n_sub)]

    def step(q, carry, kt_c, v_c):
        m, acc = carry
        s = jnp.dot(q, kt_c, preferred_element_type=F32)
        m_new = jnp.maximum(m, jnp.max(s, axis=-1, keepdims=True))
        alpha = jnp.exp2(m - m_new)
        p = jnp.exp2(s - m_new)
        acc = alpha * acc + jnp.dot(p.astype(BF16), v_c, preferred_element_type=F32)
        return m_new, acc

    carries = [(jnp.full((rows, 1), -jnp.inf, F32), jnp.zeros((rows, V_EXT), F32)) for _ in range(n_sub)]
    carries = [step(q, c, kct_ref[0, 0], vc_ref[0, 0]) for q, c in zip(qs, carries)]
    for j in range(n_tok // tk):
        carries = [step(q, c, kt_ref[0, 0, :, j * tk:(j + 1) * tk], v_ref[0, 0, j * tk:(j + 1) * tk, :])
                   for q, c in zip(qs, carries)]
    for i, (_, acc) in enumerate(carries):
        out = acc[:, :HEAD_DIM] / acc[:, HEAD_DIM:HEAD_DIM + 1]
        for g in range(Q_PER_KV):
            o_ref[i * ts:(i + 1) * ts, g * HEAD_DIM:(g + 1) * HEAD_DIM] = out[g * ts:(g + 1) * ts].astype(BF16)


def _attn(q, kt, v, kct, vc, bsz, n_tok, n_ctx):
    tq, tk = ATTN_TQ, ATTN_TK
    nq = n_tok // tq
    gw = Q_PER_KV * HEAD_DIM
    est = (2 * (Q_PER_KV * tq * HEAD_DIM * 2 + (n_tok + n_ctx) * (HEAD_DIM + V_EXT) * 2 + tq * gw * 2)
           + Q_PER_KV * tq * (tk + V_EXT) * 4 + (4 << 20))
    return pl.pallas_call(
        functools.partial(_attn_kernel, tq=tq, tk=tk, n_tok=n_tok, n_sub=ATTN_SUBTILES),
        grid=(bsz, N_KV_HEADS, nq),
        in_specs=[
            pl.BlockSpec((1, Q_PER_KV, tq, HEAD_DIM), lambda b, h, i: (b, h, i, 0)),
            pl.BlockSpec((1, 1, HEAD_DIM, n_tok), lambda b, h, i: (b, h, 0, 0)),
            pl.BlockSpec((1, 1, n_tok, V_EXT), lambda b, h, i: (b, h, 0, 0)),
            pl.BlockSpec((1, 1, HEAD_DIM, n_ctx), lambda b, h, i: (b, h, 0, 0)),
            pl.BlockSpec((1, 1, n_ctx, V_EXT), lambda b, h, i: (b, h, 0, 0)),
        ],
        out_specs=pl.BlockSpec((tq, gw), lambda b, h, i: (b * nq + i, h)),
        out_shape=jax.ShapeDtypeStruct((bsz * n_tok, ATTN_WIDTH), BF16),
        compiler_params=_cparams(("arbitrary", "arbitrary", "arbitrary"), est),
        name="attn",
    )(q, kt, v, kct, vc)


def _layer_norm(r, g, b):
    mu = jnp.mean(r, axis=-1, keepdims=True)
    rc = r - mu
    var = jnp.mean(rc * rc, axis=-1, keepdims=True)
    return rc * lax.rsqrt(var + LN_EPS) * g + b


def _pool_mix(ext_ref, uc_ref, pw_ref, ps_ref, t0, r0, sub, n_tok, gi):
    half = POOL_WINDOWS[gi] // 2
    cols = slice(gi * POOL_GROUP, (gi + 1) * POOL_GROUP)
    acc = ext_ref[HALO + r0 - half:HALO + r0 - half + sub, cols]
    for dd in range(-half + 1, half):
        acc = acc + ext_ref[HALO + r0 + dd:HALO + r0 + dd + sub, cols]
    t = t0 + r0 + lax.broadcasted_iota(jnp.int32, (sub, POOL_GROUP), 0)
    cnt = (jnp.minimum(t + half, n_tok) - jnp.maximum(t - half, 0)).astype(F32)
    diff = acc / cnt - uc_ref[r0:r0 + sub, cols]
    return jnp.dot(diff, pw_ref[0, gi], preferred_element_type=F32) * ps_ref[:, cols]


def _outproj_kernel(up_ref, uc_ref, un_ref, pw_ref, ps_ref, attn_ref, x_ref, w_ref, g1_ref, sc_ref, sh_ref,
                    lg_ref, lb_ref, rw_ref, acc_ref, h_ref, ext_ref, *, tiles_per_batch, sub, n_tok):
    b = pl.program_id(0) // tiles_per_batch
    it = pl.program_id(0) % tiles_per_batch
    tm, d = x_ref.shape
    ext_ref[0:HALO, :] = jnp.where(it == 0, 0.0, up_ref[...])
    ext_ref[HALO:HALO + tm, :] = uc_ref[...]
    ext_ref[HALO + tm:HALO + tm + HALO, :] = jnp.where(it == tiles_per_batch - 1, 0.0, un_ref[...])
    g1 = g1_ref[pl.ds(b, 1), :]
    sc = 1.0 + sc_ref[pl.ds(b, 1), :]
    sh = sh_ref[pl.ds(b, 1), :]
    rw = rw_ref[...].astype(BF16)
    for r0 in range(0, tm, sub):
        rows = slice(r0, r0 + sub)
        y = jnp.dot(attn_ref[rows, :].astype(F32), w_ref[0, POOL_WIDTH:, :], preferred_element_type=F32)
        for gi in range(len(POOL_WINDOWS)):
            mixed = _pool_mix(ext_ref, uc_ref, pw_ref, ps_ref, it * tm, r0, sub, n_tok, gi)
            y = y + jnp.dot(mixed, w_ref[0, gi * POOL_GROUP:(gi + 1) * POOL_GROUP, :], preferred_element_type=F32)
        r = ALPHA * x_ref[rows, :] + g1 * y
        xm = _layer_norm(r, lg_ref[...], lb_ref[...])
        acc_ref[rows, :] = ALPHA * xm
        h2 = xm * sc + sh
        h_ref[rows, :d] = h2
        logits = jnp.dot(h2.astype(BF16), rw, preferred_element_type=F32)
        valid = lax.broadcasted_iota(jnp.int32, logits.shape, 1) < N_EXPERTS
        lg = jnp.where(valid, logits, -jnp.inf)
        ex = jnp.exp(lg - jnp.max(lg, axis=-1, keepdims=True))
        h_ref[rows, d:] = ex / jnp.sum(ex, axis=-1, keepdims=True)


def _outproj(u, pool_w, pool_scale, attn, x2, w_out, g1, sc2, sh2, ln_g, ln_b, rw_pad, bsz, n_tok):
    tm = OUTPROJ_TM
    tpb = n_tok // tm
    d = x2.shape[1]
    n_rows = bsz * n_tok
    hb = tm // HALO
    last_hblk = n_rows // HALO - 1
    est = d * d * 4 + 2 * tm * d * 4 * 4 + 8 * tm * d * 4 + (8 << 20)
    return pl.pallas_call(
        functools.partial(_outproj_kernel, tiles_per_batch=tpb, sub=SUB_ROWS, n_tok=n_tok),
        grid=(bsz * tpb,),
        in_specs=[
            pl.BlockSpec((HALO, POOL_WIDTH), lambda i: (jnp.maximum(i * hb - 1, 0), 0)),
            pl.BlockSpec((tm, POOL_WIDTH), lambda i: (i, 0)),
            pl.BlockSpec((HALO, POOL_WIDTH), lambda i: (jnp.minimum((i + 1) * hb, last_hblk), 0)),
            pl.BlockSpec((1, len(POOL_WINDOWS), POOL_GROUP, POOL_GROUP), lambda i: (0, 0, 0, 0)),
            pl.BlockSpec((1, POOL_WIDTH), lambda i: (0, 0)),
            pl.BlockSpec((tm, ATTN_WIDTH), lambda i: (i, 0)),
            pl.BlockSpec((tm, d), lambda i: (i, 0)),
            pl.BlockSpec((1, d, d), lambda i: (0, 0, 0)),
            pl.BlockSpec((MOD_ROWS, d), lambda i: (0, 0)),
            pl.BlockSpec((MOD_ROWS, d), lambda i: (0, 0)),
            pl.BlockSpec((MOD_ROWS, d), lambda i: (0, 0)),
            pl.BlockSpec((1, d), lambda i: (0, 0)),
            pl.BlockSpec((1, d), lambda i: (0, 0)),
            pl.BlockSpec((d, LANES), lambda i: (0, 0)),
        ],
        out_specs=[
            pl.BlockSpec((tm, d), lambda i: (i, 0)),
            pl.BlockSpec((tm, d + LANES), lambda i: (i, 0)),
        ],
        out_shape=[
            jax.ShapeDtypeStruct((n_rows, d), F32),
            jax.ShapeDtypeStruct((n_rows, d + LANES), F32),
        ],
        scratch_shapes=[pltpu.VMEM((tm + 2 * HALO, POOL_WIDTH), F32)],
        compiler_params=_cparams(("arbitrary",), est),
        name="outproj",
    )(u, u, u, pool_w, pool_scale, attn, x2, w_out, g1, sc2, sh2, ln_g, ln_b, rw_pad)


def _prefix_incl(flags_bf16, out_ref, n_tok):
    blk = PREFIX_BLK
    tri = (lax.broadcasted_iota(jnp.int32, (blk, blk), 0)
           >= lax.broadcasted_iota(jnp.int32, (blk, blk), 1)).astype(BF16)
    carry = jnp.zeros((1, LANES), F32)
    for k in range(n_tok // blk):
        p = jnp.dot(tri, flags_bf16[k * blk:(k + 1) * blk], preferred_element_type=F32) + carry
        out_ref[k * blk:(k + 1) * blk, :] = p
        carry = p[blk - 1:blk, :]


def _topk_kernel(aff_ref, idx_ref, cnt_ref, part_ref, *, n_tok, cap):
    lane = lax.broadcasted_iota(jnp.int32, (1, LANES), 1)
    bits = jnp.where(lane < N_EXPERTS, pltpu.bitcast(aff_ref[...], jnp.int32), -1)

    seg = n_tok // COUNT_CHAINS

    def search(_, carry):
        lo, hi = carry
        mid = lo + ((hi - lo + 1) >> 1)
        n_ge = sum(jnp.sum((bits[k * seg:(k + 1) * seg] >= mid).astype(jnp.int32), axis=0, keepdims=True)
                   for k in range(COUNT_CHAINS))
        ok = n_ge >= cap
        return jnp.where(ok, mid, lo), jnp.where(ok, hi, mid - 1)

    lo0 = jnp.zeros((1, LANES), jnp.int32)
    hi0 = jnp.full((1, LANES), 0x7F800000, jnp.int32)
    thr, _ = lax.fori_loop(0, 32, search, (lo0, hi0))

    gt = bits > thr
    eq = bits == thr
    need = (cap - jnp.sum(gt.astype(jnp.int32), axis=0, keepdims=True)).astype(F32)
    _prefix_incl(jnp.where(eq, 1.0, 0.0).astype(BF16), cnt_ref, n_tok)
    sel = gt | (eq & (cnt_ref[...] <= need))
    _prefix_incl(jnp.where(sel, 1.0, 0.0).astype(BF16), cnt_ref, n_tok)

    slot = lax.broadcasted_iota(jnp.int32, (1, cap), 1).astype(F32)
    blk = PREFIX_BLK
    part_ref[...] = jnp.zeros_like(part_ref)

    def count(k, _):
        r0 = pl.multiple_of(k * blk, blk)
        cblk = cnt_ref[pl.ds(r0, blk), :]
        for e in range(N_EXPERTS):
            hit = (cblk[:, e:e + 1] <= slot).astype(F32)
            part_ref[e] += jnp.sum(hit.reshape(blk // SUBLANES, SUBLANES, cap), axis=0)
        return 0

    lax.fori_loop(0, n_tok // blk, count, 0)
    for e in range(N_EXPERTS):
        idx_ref[0, e:e + 1, :] = jnp.sum(part_ref[e], axis=0, keepdims=True).astype(jnp.int32)


def _topk(h2ext, bsz, n_tok, cap):
    aff_blk = h2ext.shape[1] // LANES - 1
    return pl.pallas_call(
        functools.partial(_topk_kernel, n_tok=n_tok, cap=cap),
        grid=(bsz,),
        in_specs=[pl.BlockSpec((n_tok, LANES), lambda b: (b, aff_blk))],
        out_specs=pl.BlockSpec((1, N_EXPERTS, cap), lambda b: (b, 0, 0)),
        out_shape=jax.ShapeDtypeStruct((bsz, N_EXPERTS, cap), jnp.int32),
        scratch_shapes=[pltpu.VMEM((n_tok, LANES), F32), pltpu.VMEM((N_EXPERTS, SUBLANES, cap), F32)],
        compiler_params=_cparams(("arbitrary",), 32 << 20),
        name="topk",
    )(h2ext)


def _moe_kernel(idxp_ref, idxc_ref, idxn_ref, h_hbm, acc_in_hbm, wg_ref, wu_ref, wd_ref, g2_ref, acc_hbm,
                xe_ref, xb_ref, gate_ref, y_ref, ab_ref, sem, *, n_f, n_rows, d, cap, n_x, n_sc):
    del acc_in_hbm
    e = pl.program_id(0)
    f = pl.program_id(1)
    n_e = pl.num_programs(0)
    sem_x, sem_a, sem_s = sem.at[0], sem.at[1], sem.at[2]
    rx = n_rows // n_x
    rs = n_rows // n_sc

    def x_copy(idx_ref, g0, n, k):
        return pltpu.make_async_copy(h_hbm.at[pl.ds(idx_ref[0, 0, g0 + k], 1), :],
                                     xe_ref.at[pl.ds(g0, n), :].at[pl.ds(k, 1), :], sem_x)

    def a_copy(idx_ref, g0, n, k):
        return pltpu.make_async_copy(acc_hbm.at[pl.ds(idx_ref[0, 0, g0 + k], 1), :],
                                     ab_ref.at[pl.ds(g0, n), :].at[pl.ds(k, 1), :], sem_a)

    def s_copy(idx_ref, g0, n, k):
        return pltpu.make_async_copy(ab_ref.at[pl.ds(g0, n), :].at[pl.ds(k, 1), :],
                                     acc_hbm.at[pl.ds(idx_ref[0, 0, g0 + k], 1), :], sem_s)

    def start_window(make, idx_ref, j0, n):
        g0 = pl.multiple_of(j0, SUBLANES)
        for k in range(n):
            make(idx_ref, g0, n, k).start()

    def start_all(make, idx_ref):
        def body(g, _):
            start_window(make, idx_ref, g * SUBLANES, SUBLANES)
            return 0

        lax.fori_loop(0, n_rows // SUBLANES, body, 0)

    def wait_x():
        pltpu.make_async_copy(h_hbm.at[pl.ds(0, n_rows), :], xe_ref, sem_x).wait()

    def wait_a():
        pltpu.make_async_copy(acc_hbm.at[pl.ds(0, n_rows), :], ab_ref, sem_a).wait()

    def wait_s():
        pltpu.make_async_copy(ab_ref, acc_hbm.at[pl.ds(0, n_rows), :], sem_s).wait()

    @pl.when((e == 0) & (f == 0))
    def _():
        start_all(x_copy, idxc_ref)
        start_all(a_copy, idxc_ref)
        wait_a()

    @pl.when(f == 0)
    def _():
        wait_x()
        xb_ref[...] = xe_ref[:, :d].astype(BF16)
        lane = lax.broadcasted_iota(jnp.int32, (n_rows, LANES), 1)
        gate_ref[...] = jnp.sum(jnp.where(lane == e, xe_ref[:, d:], 0.0), axis=-1, keepdims=True)
        y_ref[...] = jnp.zeros_like(y_ref)

    @pl.when(f == n_sc + 1)
    def _():
        wait_s()

    def compute():
        x = xb_ref[...]
        a = jnp.dot(x, wg_ref[0, 0].astype(BF16), preferred_element_type=F32)
        b = jnp.dot(x, wu_ref[0, 0].astype(BF16), preferred_element_type=F32)
        hm = (a * _sigmoid(a) * b).astype(BF16)
        y_ref[...] += jnp.dot(hm, wd_ref[0, 0].astype(BF16), preferred_element_type=F32)

    def step_plain():
        compute()

    def step_scatter():
        start_window(x_copy, idxn_ref, (f - 1) * rx, rx)
        start_window(s_copy, idxp_ref, (f - 1) * rs, rs)
        compute()

    def step_gather():
        start_window(x_copy, idxn_ref, (f - 1) * rx, rx)
        start_window(a_copy, idxc_ref, (f - n_sc - 1) * rs, rs)
        compute()

    phase = jnp.where((f >= 1) & (f <= n_sc), 1, jnp.where((f > n_sc) & (f <= 2 * n_sc), 2, 0))
    lax.switch(phase, [step_plain, step_scatter, step_gather])

    @pl.when(f == n_f - 1)
    def _():
        wait_a()
        for bi in range(n_rows // cap):
            rows = slice(bi * cap, (bi + 1) * cap)
            ab_ref[rows, :] += (gate_ref[rows, :] * g2_ref[bi:bi + 1, :]) * y_ref[rows, :]

        @pl.when(e == n_e - 1)
        def _():
            wait_x()
            start_all(s_copy, idxc_ref)
            wait_s()


def _moe(idx_rows, h2ext, acc0, w_gate, w_up, w_down, g2, cap):
    d = acc0.shape[1]
    n_rows = idx_rows.shape[2]
    ff = w_gate.shape[3]
    tf = MOE_TF
    n_f = ff // tf
    n_x, n_sc = MOE_XGATHER_STEPS, MOE_SCATTER_STEPS
    assert n_x == 2 * n_sc and n_x + 1 <= n_f - 1 and n_rows % (n_x * SUBLANES) == 0
    est = (2 * 3 * d * tf * 4 + n_rows * h2ext.shape[1] * 4 + n_rows * d * 2 + n_rows * LANES * 4
           + 2 * n_rows * d * 4 + 3 * d * tf * 2 + 4 * n_rows * tf * 4 + (6 << 20))
    last = N_EXPERTS - 1
    return pl.pallas_call(
        functools.partial(_moe_kernel, n_f=n_f, n_rows=n_rows, d=d, cap=cap, n_x=n_x, n_sc=n_sc),
        grid=(N_EXPERTS, n_f),
        in_specs=[
            pl.BlockSpec((1, 1, n_rows), lambda e, f: (jnp.maximum(e - 1, 0), 0, 0), memory_space=pltpu.SMEM),
            pl.BlockSpec((1, 1, n_rows), lambda e, f: (e, 0, 0), memory_space=pltpu.SMEM),
            pl.BlockSpec((1, 1, n_rows), lambda e, f: (jnp.minimum(e + 1, last), 0, 0), memory_space=pltpu.SMEM),
            pl.BlockSpec(memory_space=pl.ANY),
            pl.BlockSpec(memory_space=pl.ANY),
            pl.BlockSpec((1, 1, d, tf), lambda e, f: (0, e, 0, f)),
            pl.BlockSpec((1, 1, d, tf), lambda e, f: (0, e, 0, f)),
            pl.BlockSpec((1, 1, tf, d), lambda e, f: (0, e, f, 0)),
            pl.BlockSpec((MOD_ROWS, d), lambda e, f: (0, 0)),
        ],
        out_specs=pl.BlockSpec(memory_space=pl.ANY),
        out_shape=jax.ShapeDtypeStruct(acc0.shape, F32),
        scratch_shapes=[
            pltpu.VMEM((n_rows, h2ext.shape[1]), F32),
            pltpu.VMEM((n_rows, d), BF16),
            pltpu.VMEM((n_rows, 1), F32),
            pltpu.VMEM((n_rows, d), F32),
            pltpu.VMEM((n_rows, d), F32),
            pltpu.SemaphoreType.DMA((3,)),
        ],
        input_output_aliases={4: 0},
        compiler_params=_cparams(("arbitrary", "arbitrary"), est),
        name="moe",
    )(idx_rows, idx_rows, idx_rows, h2ext, acc0, w_gate, w_up, w_down, g2)


def _final_kernel(a_ref, g_ref, b_ref, o_ref):
    o_ref[...] = _layer_norm(a_ref[...], g_ref[...], b_ref[...])


def _final(acc, ln_g, ln_b):
    n_rows, d = acc.shape
    tm = FINAL_TM
    return pl.pallas_call(
        _final_kernel,
        grid=(n_rows // tm,),
        in_specs=[
            pl.BlockSpec((tm, d), lambda i: (i, 0)),
            pl.BlockSpec((1, d), lambda i: (0, 0)),
            pl.BlockSpec((1, d), lambda i: (0, 0)),
        ],
        out_specs=pl.BlockSpec((tm, d), lambda i: (i, 0)),
        out_shape=jax.ShapeDtypeStruct((n_rows, d), F32),
        compiler_params=_cparams(("arbitrary",), 8 * tm * d * 4 + (8 << 20)),
        name="final_ln",
    )(acc, ln_g, ln_b)


def _rope_tables(n_tok):
    rows = n_tok // GRID_W
    row = np.broadcast_to(np.arange(rows, dtype=np.float64)[:, None], (rows, GRID_W)).reshape(-1)
    col = np.broadcast_to(np.arange(GRID_W, dtype=np.float64)[None, :], (rows, GRID_W)).reshape(-1)
    inv_freq = ROPE_THETA ** (-np.arange(0, ROPE_AXIS_DIM, 2, dtype=np.float64) / ROPE_AXIS_DIM)
    ang_r = row[:, None] * inv_freq[None, :]
    ang_c = col[:, None] * inv_freq[None, :]
    ang = np.concatenate([ang_r, ang_r, ang_c, ang_c], axis=-1)
    cos, sin = np.cos(ang).astype(np.float32), np.sin(ang).astype(np.float32)
    low = (np.arange(HEAD_DIM) % ROPE_AXIS_DIM) < ROPE_AXIS_DIM // 2
    sin_lo = np.where(low[None, :], -sin, np.float32(0.0))
    sin_hi = np.where(low[None, :], np.float32(0.0), sin)
    return jnp.asarray(cos), jnp.asarray(sin_lo), jnp.asarray(sin_hi)


def kernel(x, c, ctx, c_ctx, ada_w, ada_b, w_in, q_norm, k_norm, pool_w, pool_scale, w_out,
           ln1_g, ln1_b, router_w, w_gate, w_up, w_down, ln2_g, ln2_b):
    bsz, n_tok, d = x.shape
    n_ctx = ctx.shape[1]
    assert ada_w.shape[0] == DEPTH and d == D_MODEL and bsz + 1 <= MOD_ROWS
    cap = CAPACITY_FACTOR * n_tok // N_EXPERTS

    c8 = jnp.concatenate([c, c_ctx[None, :], jnp.zeros((MOD_ROWS - bsz - 1, d), F32)], axis=0)
    mod = _ada(c8, ada_w, ada_b)
    sh1, sc1, g1, sh2, sc2, g2 = [mod[:, k * d:(k + 1) * d] for k in range(6)]

    x2 = x.reshape(bsz * n_tok, d)
    cos, sin_lo, sin_hi = _rope_tables(n_tok)
    u, q, kt, v = _inproj(x2, sc1, sh1, w_in, q_norm, k_norm, cos, sin_lo, sin_hi, bsz, n_tok)
    kct, vc = _ctxkv(ctx.reshape(bsz * n_ctx, d), sc1, sh1, w_in, k_norm, bsz, n_ctx)

    attn = _attn(q, kt, v, kct, vc, bsz, n_tok, n_ctx)

    rw_pad = jnp.pad(router_w[0], ((0, 0), (0, LANES - N_EXPERTS)))
    acc0, h2ext = _outproj(u, pool_w, pool_scale, attn, x2, w_out, g1, sc2, sh2, ln1_g, ln1_b, rw_pad,
                           bsz, n_tok)

    idx = _topk(h2ext, bsz, n_tok, cap)
    idx_rows = (idx + (jnp.arange(bsz, dtype=jnp.int32) * n_tok)[:, None, None])
    idx_rows = idx_rows.transpose(1, 0, 2).reshape(N_EXPERTS, 1, bsz * cap)

    acc = _moe(idx_rows, h2ext, acc0, w_gate, w_up, w_down, g2, cap)
    out = _final(acc, ln2_g, ln2_b)
    return out.reshape(bsz, n_tok, d)
```

```python
import functools

import jax
import jax.numpy as jnp
import numpy as np
from jax import lax
from jax.experimental import pallas as pl
from jax.experimental.pallas import tpu as pltpu

F32 = jnp.float32
BF16 = jnp.bfloat16

D_MODEL = 2048
GRID_W = 64
HEAD_DIM = 128
ATTN_WIDTH = D_MODEL // 2
N_HEADS = ATTN_WIDTH // HEAD_DIM
N_KV_HEADS = 2
Q_PER_KV = N_HEADS // N_KV_HEADS
KV_WIDTH = N_KV_HEADS * HEAD_DIM
POOL_WIDTH = D_MODEL - ATTN_WIDTH
POOL_WINDOWS = (2, 4, 8, 16)
POOL_GROUP = POOL_WIDTH // len(POOL_WINDOWS)
Q_OFF = POOL_WIDTH
K_OFF = Q_OFF + ATTN_WIDTH
V_OFF = K_OFF + KV_WIDTH
IN_WIDTH = V_OFF + KV_WIDTH
ROPE_AXIS_DIM = HEAD_DIM // 2
ROPE_THETA = 10000.0
ATTN_SCALE = HEAD_DIM ** -0.5
Q_SCALE = ATTN_SCALE * float(np.log2(np.e))
V_EXT = 2 * HEAD_DIM
N_EXPERTS = 16
CAPACITY_FACTOR = 2
LN_EPS = 1e-6
QK_EPS = 1e-6
DEPTH = 1
ALPHA = (2.0 * DEPTH) ** 0.25

LANES = 128
SUBLANES = 8
V7X_VMEM_BYTES = 64 * 1024 * 1024
MOD_ROWS = SUBLANES
HALO = SUBLANES

ADA_TN = 1024
SUB_ROWS = 256
INPROJ_TM = 512
ATTN_TQ = 512
ATTN_SUBTILES = 2
ATTN_TK = 512
OUTPROJ_TM = 512
MOE_TF = 512
MOE_XGATHER_STEPS = 8
MOE_SCATTER_STEPS = 4
FINAL_TM = 512
PREFIX_BLK = 256
COUNT_CHAINS = 8


def _vmem_limit(nbytes):
    return int(min(nbytes, V7X_VMEM_BYTES - 4 * 1024 * 1024))


def _cparams(sem, nbytes):
    return pltpu.CompilerParams(dimension_semantics=sem, vmem_limit_bytes=_vmem_limit(nbytes))


def _sigmoid(x):
    return 1.0 / (1.0 + jnp.exp(-x))


def _ada_kernel(c_ref, w_ref, b_ref, o_ref):
    cv = c_ref[...]
    s = cv * _sigmoid(cv)
    o_ref[...] = jnp.dot(s, w_ref[0], preferred_element_type=F32) + b_ref[...]


def _ada(c8, ada_w, ada_b):
    d = c8.shape[1]
    n_out = ada_w.shape[2]
    return pl.pallas_call(
        _ada_kernel,
        grid=(n_out // ADA_TN,),
        in_specs=[
            pl.BlockSpec((MOD_ROWS, d), lambda j: (0, 0)),
            pl.BlockSpec((1, d, ADA_TN), lambda j: (0, 0, j)),
            pl.BlockSpec((1, ADA_TN), lambda j: (0, j)),
        ],
        out_specs=pl.BlockSpec((MOD_ROWS, ADA_TN), lambda j: (0, j)),
        out_shape=jax.ShapeDtypeStruct((MOD_ROWS, n_out), F32),
        compiler_params=_cparams(("arbitrary",), 2 * d * ADA_TN * 4 + (8 << 20)),
        name="ada",
    )(c8, ada_w, ada_b)


def _ones_column(n):
    lane = lax.broadcasted_iota(jnp.int32, (n, V_EXT - HEAD_DIM), 1)
    return jnp.where(lane == 0, 1.0, 0.0)


def _rms(xh, g):
    ms = jnp.mean(xh * xh, axis=-1, keepdims=True)
    return xh * lax.rsqrt(ms + QK_EPS) * g


def _rope(xn, cos, sin_lo, sin_hi):
    return (xn * cos + pltpu.roll(xn, HEAD_DIM - ROPE_AXIS_DIM // 2, 1) * sin_lo
            + pltpu.roll(xn, ROPE_AXIS_DIM // 2, 1) * sin_hi)


def _inproj_kernel(x_ref, sc_ref, sh_ref, w_ref, qn_ref, kn_ref, cos_ref, slo_ref, shi_ref,
                   u_ref, q_ref, kt_ref, v_ref, *, tiles_per_batch, sub):
    b = pl.program_id(0) // tiles_per_batch
    sc = 1.0 + sc_ref[pl.ds(b, 1), :]
    sh = sh_ref[pl.ds(b, 1), :]
    qn = qn_ref[...]
    kn = kn_ref[...]
    for r0 in range(0, x_ref.shape[0], sub):
        rows = slice(r0, r0 + sub)
        h = x_ref[rows, :] * sc + sh
        proj = jnp.dot(h, w_ref[0], preferred_element_type=F32)
        u_ref[rows, :] = proj[:, :Q_OFF]
        cos = cos_ref[rows, :]
        slo = slo_ref[rows, :]
        shi = shi_ref[rows, :]
        for hd in range(N_HEADS):
            xh = proj[:, Q_OFF + hd * HEAD_DIM:Q_OFF + (hd + 1) * HEAD_DIM]
            qr = _rope(_rms(xh, qn), cos, slo, shi) * Q_SCALE
            q_ref[0, hd, rows, :] = qr
        for hd in range(N_KV_HEADS):
            kh = proj[:, K_OFF + hd * HEAD_DIM:K_OFF + (hd + 1) * HEAD_DIM]
            kr = _rope(_rms(kh, kn), cos, slo, shi)
            kt_ref[0, hd, :, rows] = kr.T
            v_ref[0, hd, rows, :HEAD_DIM] = proj[:, V_OFF + hd * HEAD_DIM:V_OFF + (hd + 1) * HEAD_DIM]
            v_ref[0, hd, rows, HEAD_DIM:] = _ones_column(sub)


def _inproj(x2, sc1, sh1, w_in, q_norm, k_norm, cos, sin_lo, sin_hi, bsz, n_tok):
    tm = INPROJ_TM
    tpb = n_tok // tm
    d = x2.shape[1]
    est = (2 * tm * d * 4 + d * IN_WIDTH * 4 + 2 * tm * IN_WIDTH * 4 + 2 * tm * IN_WIDTH * 4
           + (8 << 20))
    return pl.pallas_call(
        functools.partial(_inproj_kernel, tiles_per_batch=tpb, sub=SUB_ROWS),
        grid=(bsz * tpb,),
        in_specs=[
            pl.BlockSpec((tm, d), lambda i: (i, 0)),
            pl.BlockSpec((MOD_ROWS, d), lambda i: (0, 0)),
            pl.BlockSpec((MOD_ROWS, d), lambda i: (0, 0)),
            pl.BlockSpec((1, d, IN_WIDTH), lambda i: (0, 0, 0)),
            pl.BlockSpec((1, HEAD_DIM), lambda i: (0, 0)),
            pl.BlockSpec((1, HEAD_DIM), lambda i: (0, 0)),
            pl.BlockSpec((tm, HEAD_DIM), lambda i: (i % tpb, 0)),
            pl.BlockSpec((tm, HEAD_DIM), lambda i: (i % tpb, 0)),
            pl.BlockSpec((tm, HEAD_DIM), lambda i: (i % tpb, 0)),
        ],
        out_specs=[
            pl.BlockSpec((tm, POOL_WIDTH), lambda i: (i, 0)),
            pl.BlockSpec((1, N_HEADS, tm, HEAD_DIM), lambda i: (i // tpb, 0, i % tpb, 0)),
            pl.BlockSpec((1, N_KV_HEADS, HEAD_DIM, tm), lambda i: (i // tpb, 0, 0, i % tpb)),
            pl.BlockSpec((1, N_KV_HEADS, tm, V_EXT), lambda i: (i // tpb, 0, i % tpb, 0)),
        ],
        out_shape=[
            jax.ShapeDtypeStruct((bsz * n_tok, POOL_WIDTH), F32),
            jax.ShapeDtypeStruct((bsz, N_HEADS, n_tok, HEAD_DIM), F32),
            jax.ShapeDtypeStruct((bsz, N_KV_HEADS, HEAD_DIM, n_tok), F32),
            jax.ShapeDtypeStruct((bsz, N_KV_HEADS, n_tok, V_EXT), F32),
        ],
        compiler_params=_cparams(("arbitrary",), est),
        name="inproj",
    )(x2, sc1, sh1, w_in, q_norm, k_norm, cos, sin_lo, sin_hi)


def _ctxkv_kernel(ctx_ref, sc_ref, sh_ref, w_ref, kn_ref, kt_ref, v_ref, *, bsz, n_ctx):
    hc = ctx_ref[...] * (1.0 + sc_ref[2:3, :]) + sh_ref[2:3, :]
    kv = jnp.dot(hc, w_ref[0], preferred_element_type=F32)
    kn = kn_ref[...]
    for b in range(bsz):
        rows = slice(b * n_ctx, (b + 1) * n_ctx)
        for hd in range(N_KV_HEADS):
            kh = kv[rows, hd * HEAD_DIM:(hd + 1) * HEAD_DIM]
            kt_ref[b, hd] = _rms(kh, kn).T
            v_ref[b, hd, :, :HEAD_DIM] = kv[rows, KV_WIDTH + hd * HEAD_DIM:KV_WIDTH + (hd + 1) * HEAD_DIM]
            v_ref[b, hd, :, HEAD_DIM:] = _ones_column(n_ctx)


def _ctxkv(ctx2, sc1, sh1, w_in, k_norm, bsz, n_ctx):
    d = ctx2.shape[1]
    kvw = 2 * KV_WIDTH
    return pl.pallas_call(
        functools.partial(_ctxkv_kernel, bsz=bsz, n_ctx=n_ctx),
        grid=(1,),
        in_specs=[
            pl.BlockSpec((bsz * n_ctx, d), lambda i: (0, 0)),
            pl.BlockSpec((MOD_ROWS, d), lambda i: (0, 0)),
            pl.BlockSpec((MOD_ROWS, d), lambda i: (0, 0)),
            pl.BlockSpec((1, d, kvw), lambda i: (0, 0, K_OFF // kvw)),
            pl.BlockSpec((1, HEAD_DIM), lambda i: (0, 0)),
        ],
        out_specs=[
            pl.BlockSpec((bsz, N_KV_HEADS, HEAD_DIM, n_ctx), lambda i: (0, 0, 0, 0)),
            pl.BlockSpec((bsz, N_KV_HEADS, n_ctx, V_EXT), lambda i: (0, 0, 0, 0)),
        ],
        out_shape=[
            jax.ShapeDtypeStruct((bsz, N_KV_HEADS, HEAD_DIM, n_ctx), F32),
            jax.ShapeDtypeStruct((bsz, N_KV_HEADS, n_ctx, V_EXT), F32),
        ],
        compiler_params=_cparams(("arbitrary",), 2 * (bsz * n_ctx * d * 4 + d * kvw * 4) + (8 << 20)),
        name="ctxkv",
    )(ctx2, sc1, sh1, w_in, k_norm)


def _attn_kernel(q_ref, kt_ref, v_ref, kct_ref, vc_ref, o_ref, *, tq, tk, n_tok, n_sub):
    ts = tq // n_sub
    rows = Q_PER_KV * ts
    qs = [q_ref[0, :, i * ts:(i + 1) * ts, :].reshape(rows, HEAD_DIM) for i in range(n_sub)]

    def step(q, carry, kt_c, v_c):
        m, acc = carry
        s = jnp.dot(q, kt_c, preferred_element_type=F32)
        m_new = jnp.maximum(m, jnp.max(s, axis=-1, keepdims=True))
        alpha = jnp.exp2(m - m_new)
        p = jnp.exp2(s - m_new)
        acc = alpha * acc + jnp.dot(p, v_c, preferred_element_type=F32)
        return m_new, acc

    carries = [(jnp.full((rows, 1), -jnp.inf, F32), jnp.zeros((rows, V_EXT), F32)) for _ in range(n_sub)]
    carries = [step(q, c, kct_ref[0, 0], vc_ref[0, 0]) for q, c in zip(qs, carries)]
    for j in range(n_tok // tk):
        carries = [step(q, c, kt_ref[0, 0, :, j * tk:(j + 1) * tk], v_ref[0, 0, j * tk:(j + 1) * tk, :])
                   for q, c in zip(qs, carries)]
    for i, (_, acc) in enumerate(carries):
        out = acc[:, :HEAD_DIM] / acc[:, HEAD_DIM:HEAD_DIM + 1]
        for g in range(Q_PER_KV):
            o_ref[i * ts:(i + 1) * ts, g * HEAD_DIM:(g + 1) * HEAD_DIM] = out[g * ts:(g + 1) * ts]


def _attn(q, kt, v, kct, vc, bsz, n_tok, n_ctx):
    tq, tk = ATTN_TQ, ATTN_TK
    nq = n_tok // tq
    gw = Q_PER_KV * HEAD_DIM
    return pl.pallas_call(
        functools.partial(_attn_kernel, tq=tq, tk=tk, n_tok=n_tok, n_sub=ATTN_SUBTILES),
        grid=(bsz, N_KV_HEADS, nq),
        in_specs=[
            pl.BlockSpec((1, Q_PER_KV, tq, HEAD_DIM), lambda b, h, i: (b, h, i, 0)),
            pl.BlockSpec((1, 1, HEAD_DIM, n_tok), lambda b, h, i: (b, h, 0, 0)),
            pl.BlockSpec((1, 1, n_tok, V_EXT), lambda b, h, i: (b, h, 0, 0)),
            pl.BlockSpec((1, 1, HEAD_DIM, n_ctx), lambda b, h, i: (b, h, 0, 0)),
            pl.BlockSpec((1, 1, n_ctx, V_EXT), lambda b, h, i: (b, h, 0, 0)),
        ],
        out_specs=pl.BlockSpec((tq, gw), lambda b, h, i: (b * nq + i, h)),
        out_shape=jax.ShapeDtypeStruct((bsz * n_tok, ATTN_WIDTH), F32),
        compiler_params=_cparams(("arbitrary", "arbitrary", "arbitrary"), 40 << 20),
        name="attn",
    )(q, kt, v, kct, vc)


def _layer_norm(r, g, b):
    mu = jnp.mean(r, axis=-1, keepdims=True)
    rc = r - mu
    var = jnp.mean(rc * rc, axis=-1, keepdims=True)
    return rc * lax.rsqrt(var + LN_EPS) * g + b


def _window_sums(ext_ref, q_ref, r0, sub, cols, w):
    if w == 2:
        return ext_ref[HALO + r0 - 1:HALO + r0 - 1 + sub, cols] + ext_ref[HALO + r0:HALO + r0 + sub, cols]
    l2, l4, l8 = sub + 3 * HALO, sub + 2 * HALO, sub + HALO
    q_ref[0, 0:l2, :] = ext_ref[r0:r0 + l2, cols] + ext_ref[r0 + 1:r0 + 1 + l2, cols]
    q_ref[1, 0:l4, :] = q_ref[0, 0:l4, :] + q_ref[0, 2:2 + l4, :]
    if w == 4:
        return q_ref[1, HALO - 2:HALO - 2 + sub, :]
    q_ref[2, 0:l8, :] = q_ref[1, 0:l8, :] + q_ref[1, 4:4 + l8, :]
    if w == 8:
        return q_ref[2, HALO - 4:HALO - 4 + sub, :]
    assert w == 2 * HALO
    return q_ref[2, 0:sub, :] + q_ref[2, HALO:HALO + sub, :]


def _pool_mix(ext_ref, q_ref, uc_ref, pw_ref, ps_ref, t0, r0, sub, n_tok, gi):
    half = POOL_WINDOWS[gi] // 2
    cols = slice(gi * POOL_GROUP, (gi + 1) * POOL_GROUP)
    acc = _window_sums(ext_ref, q_ref, r0, sub, cols, POOL_WINDOWS[gi])
    t = t0 + r0 + lax.broadcasted_iota(jnp.int32, (sub, POOL_GROUP), 0)
    cnt = (jnp.minimum(t + half, n_tok) - jnp.maximum(t - half, 0)).astype(F32)
    diff = acc / cnt - uc_ref[r0:r0 + sub, cols]
    return jnp.dot(diff, pw_ref[0, gi], preferred_element_type=F32) * ps_ref[:, cols]


def _outproj_kernel(up_ref, uc_ref, un_ref, pw_ref, ps_ref, attn_ref, x_ref, w_ref, g1_ref, sc_ref, sh_ref,
                    lg_ref, lb_ref, rw_ref, acc_ref, h_ref, ext_ref, q_ref, *, tiles_per_batch, sub, n_tok):
    b = pl.program_id(0) // tiles_per_batch
    it = pl.program_id(0) % tiles_per_batch
    tm, d = x_ref.shape
    ext_ref[0:HALO, :] = jnp.where(it == 0, 0.0, up_ref[...])
    ext_ref[HALO:HALO + tm, :] = uc_ref[...]
    ext_ref[HALO + tm:HALO + tm + HALO, :] = jnp.where(it == tiles_per_batch - 1, 0.0, un_ref[...])
    ext_ref[tm + 2 * HALO:, :] = jnp.zeros((ext_ref.shape[0] - tm - 2 * HALO, POOL_WIDTH), F32)
    g1 = g1_ref[pl.ds(b, 1), :]
    sc = 1.0 + sc_ref[pl.ds(b, 1), :]
    sh = sh_ref[pl.ds(b, 1), :]
    rw = rw_ref[...]
    for r0 in range(0, tm, sub):
        rows = slice(r0, r0 + sub)
        a = attn_ref[rows, :]
        y = jnp.concatenate([jnp.dot(a, w_ref[0, POOL_WIDTH:, k * (d // 2):(k + 1) * (d // 2)],
                                     preferred_element_type=F32) for k in range(2)], axis=1)
        for gi in range(len(POOL_WINDOWS)):
            mixed = _pool_mix(ext_ref, q_ref, uc_ref, pw_ref, ps_ref, it * tm, r0, sub, n_tok, gi)
            y = y + jnp.dot(mixed, w_ref[0, gi * POOL_GROUP:(gi + 1) * POOL_GROUP, :], preferred_element_type=F32)
        r = ALPHA * x_ref[rows, :] + g1 * y
        xm = _layer_norm(r, lg_ref[...], lb_ref[...])
        acc_ref[rows, :] = ALPHA * xm
        h2 = xm * sc + sh
        h_ref[rows, :d] = h2
        logits = jnp.dot(h2, rw, preferred_element_type=F32)
        valid = lax.broadcasted_iota(jnp.int32, logits.shape, 1) < N_EXPERTS
        lg = jnp.where(valid, logits, -jnp.inf)
        ex = jnp.exp(lg - jnp.max(lg, axis=-1, keepdims=True))
        h_ref[rows, d:] = ex / jnp.sum(ex, axis=-1, keepdims=True)


def _outproj(u, pool_w, pool_scale, attn, x2, w_out, g1, sc2, sh2, ln_g, ln_b, rw_pad, bsz, n_tok):
    tm = OUTPROJ_TM
    tpb = n_tok // tm
    d = x2.shape[1]
    n_rows = bsz * n_tok
    hb = tm // HALO
    last_hblk = n_rows // HALO - 1
    est = d * d * 4 + 2 * tm * d * 4 * 4 + 8 * tm * d * 4 + (8 << 20)
    return pl.pallas_call(
        functools.partial(_outproj_kernel, tiles_per_batch=tpb, sub=SUB_ROWS, n_tok=n_tok),
        grid=(bsz * tpb,),
        in_specs=[
            pl.BlockSpec((HALO, POOL_WIDTH), lambda i: (jnp.maximum(i * hb - 1, 0), 0)),
            pl.BlockSpec((tm, POOL_WIDTH), lambda i: (i, 0)),
            pl.BlockSpec((HALO, POOL_WIDTH), lambda i: (jnp.minimum((i + 1) * hb, last_hblk), 0)),
            pl.BlockSpec((1, len(POOL_WINDOWS), POOL_GROUP, POOL_GROUP), lambda i: (0, 0, 0, 0)),
            pl.BlockSpec((1, POOL_WIDTH), lambda i: (0, 0)),
            pl.BlockSpec((tm, ATTN_WIDTH), lambda i: (i, 0)),
            pl.BlockSpec((tm, d), lambda i: (i, 0)),
            pl.BlockSpec((1, d, d), lambda i: (0, 0, 0)),
            pl.BlockSpec((MOD_ROWS, d), lambda i: (0, 0)),
            pl.BlockSpec((MOD_ROWS, d), lambda i: (0, 0)),
            pl.BlockSpec((MOD_ROWS, d), lambda i: (0, 0)),
            pl.BlockSpec((1, d), lambda i: (0, 0)),
            pl.BlockSpec((1, d), lambda i: (0, 0)),
            pl.BlockSpec((d, LANES), lambda i: (0, 0)),
        ],
        out_specs=[
            pl.BlockSpec((tm, d), lambda i: (i, 0)),
            pl.BlockSpec((tm, d + LANES), lambda i: (i, 0)),
        ],
        out_shape=[
            jax.ShapeDtypeStruct((n_rows, d), F32),
            jax.ShapeDtypeStruct((n_rows, d + LANES), F32),
        ],
        scratch_shapes=[pltpu.VMEM((tm + 4 * HALO, POOL_WIDTH), F32),
                        pltpu.VMEM((3, SUB_ROWS + 3 * HALO, POOL_GROUP), F32)],
        compiler_params=_cparams(("arbitrary",), est),
        name="outproj",
    )(u, u, u, pool_w, pool_scale, attn, x2, w_out, g1, sc2, sh2, ln_g, ln_b, rw_pad)


def _prefix_incl(flags_bf16, out_ref, n_tok):
    blk = PREFIX_BLK
    tri = (lax.broadcasted_iota(jnp.int32, (blk, blk), 0)
           >= lax.broadcasted_iota(jnp.int32, (blk, blk), 1)).astype(BF16)
    carry = jnp.zeros((1, LANES), F32)
    for k in range(n_tok // blk):
        p = jnp.dot(tri, flags_bf16[k * blk:(k + 1) * blk], preferred_element_type=F32) + carry
        out_ref[k * blk:(k + 1) * blk, :] = p
        carry = p[blk - 1:blk, :]


def _topk_kernel(aff_ref, idx_ref, cnt_ref, part_ref, *, n_tok, cap):
    lane = lax.broadcasted_iota(jnp.int32, (1, LANES), 1)
    bits = jnp.where(lane < N_EXPERTS, pltpu.bitcast(aff_ref[...], jnp.int32), -1)

    seg = n_tok // COUNT_CHAINS

    def search(_, carry):
        lo, hi = carry
        mid = lo + ((hi - lo + 1) >> 1)
        n_ge = sum(jnp.sum((bits[k * seg:(k + 1) * seg] >= mid).astype(jnp.int32), axis=0, keepdims=True)
                   for k in range(COUNT_CHAINS))
        ok = n_ge >= cap
        return jnp.where(ok, mid, lo), jnp.where(ok, hi, mid - 1)

    lo0 = jnp.zeros((1, LANES), jnp.int32)
    hi0 = jnp.full((1, LANES), 0x7F800000, jnp.int32)
    thr, _ = lax.fori_loop(0, 32, search, (lo0, hi0))

    gt = bits > thr
    eq = bits == thr
    need = (cap - jnp.sum(gt.astype(jnp.int32), axis=0, keepdims=True)).astype(F32)
    _prefix_incl(jnp.where(eq, 1.0, 0.0).astype(BF16), cnt_ref, n_tok)
    sel = gt | (eq & (cnt_ref[...] <= need))
    _prefix_incl(jnp.where(sel, 1.0, 0.0).astype(BF16), cnt_ref, n_tok)

    slot = lax.broadcasted_iota(jnp.int32, (1, cap), 1).astype(F32)
    blk = PREFIX_BLK
    part_ref[...] = jnp.zeros_like(part_ref)

    def count(k, _):
        r0 = pl.multiple_of(k * blk, blk)
        cblk = cnt_ref[pl.ds(r0, blk), :]
        for e in range(N_EXPERTS):
            hit = (cblk[:, e:e + 1] <= slot).astype(F32)
            part_ref[e] += jnp.sum(hit.reshape(blk // SUBLANES, SUBLANES, cap), axis=0)
        return 0

    lax.fori_loop(0, n_tok // blk, count, 0)
    for e in range(N_EXPERTS):
        idx_ref[0, e:e + 1, :] = jnp.sum(part_ref[e], axis=0, keepdims=True).astype(jnp.int32)


def _topk(h2ext, bsz, n_tok, cap):
    aff_blk = h2ext.shape[1] // LANES - 1
    return pl.pallas_call(
        functools.partial(_topk_kernel, n_tok=n_tok, cap=cap),
        grid=(bsz,),
        in_specs=[pl.BlockSpec((n_tok, LANES), lambda b: (b, aff_blk))],
        out_specs=pl.BlockSpec((1, N_EXPERTS, cap), lambda b: (b, 0, 0)),
        out_shape=jax.ShapeDtypeStruct((bsz, N_EXPERTS, cap), jnp.int32),
        scratch_shapes=[pltpu.VMEM((n_tok, LANES), F32), pltpu.VMEM((N_EXPERTS, SUBLANES, cap), F32)],
        compiler_params=_cparams(("arbitrary",), 32 << 20),
        name="topk",
    )(h2ext)


def _moe_kernel(idxp_ref, idxc_ref, idxn_ref, h_hbm, acc_in_hbm, wg_ref, wu_ref, wd_ref, g2_ref, acc_hbm,
                xe_ref, xb_ref, gate_ref, y_ref, ab_ref, sem, *, n_f, n_rows, d, cap, n_x, n_sc):
    del acc_in_hbm
    e = pl.program_id(0)
    f = pl.program_id(1)
    n_e = pl.num_programs(0)
    sem_x, sem_a, sem_s = sem.at[0], sem.at[1], sem.at[2]
    rx = n_rows // n_x
    rs = n_rows // n_sc

    def x_copy(idx_ref, g0, n, k):
        return pltpu.make_async_copy(h_hbm.at[pl.ds(idx_ref[0, 0, g0 + k], 1), :],
                                     xe_ref.at[pl.ds(g0, n), :].at[pl.ds(k, 1), :], sem_x)

    def a_copy(idx_ref, g0, n, k):
        return pltpu.make_async_copy(acc_hbm.at[pl.ds(idx_ref[0, 0, g0 + k], 1), :],
                                     ab_ref.at[pl.ds(g0, n), :].at[pl.ds(k, 1), :], sem_a)

    def s_copy(idx_ref, g0, n, k):
        return pltpu.make_async_copy(ab_ref.at[pl.ds(g0, n), :].at[pl.ds(k, 1), :],
                                     acc_hbm.at[pl.ds(idx_ref[0, 0, g0 + k], 1), :], sem_s)

    def start_window(make, idx_ref, j0, n):
        g0 = pl.multiple_of(j0, SUBLANES)
        for k in range(n):
            make(idx_ref, g0, n, k).start()

    def start_all(make, idx_ref):
        def body(g, _):
            start_window(make, idx_ref, g * SUBLANES, SUBLANES)
            return 0

        lax.fori_loop(0, n_rows // SUBLANES, body, 0)

    def wait_x():
        pltpu.make_async_copy(h_hbm.at[pl.ds(0, n_rows), :], xe_ref, sem_x).wait()

    def wait_a():
        pltpu.make_async_copy(acc_hbm.at[pl.ds(0, n_rows), :], ab_ref, sem_a).wait()

    def wait_s():
        pltpu.make_async_copy(ab_ref, acc_hbm.at[pl.ds(0, n_rows), :], sem_s).wait()

    @pl.when((e == 0) & (f == 0))
    def _():
        start_all(x_copy, idxc_ref)
        start_all(a_copy, idxc_ref)
        wait_a()

    @pl.when(f == 0)
    def _():
        wait_x()
        xb_ref[...] = xe_ref[:, :d].astype(BF16)
        lane = lax.broadcasted_iota(jnp.int32, (n_rows, LANES), 1)
        gate_ref[...] = jnp.sum(jnp.where(lane == e, xe_ref[:, d:], 0.0), axis=-1, keepdims=True)
        y_ref[...] = jnp.zeros_like(y_ref)

    @pl.when(f == n_sc + 1)
    def _():
        wait_s()

    def compute():
        x = xb_ref[...]
        a = jnp.dot(x, wg_ref[0, 0].astype(BF16), preferred_element_type=F32)
        b = jnp.dot(x, wu_ref[0, 0].astype(BF16), preferred_element_type=F32)
        hm = (a * _sigmoid(a) * b).astype(BF16)
        y_ref[...] += jnp.dot(hm, wd_ref[0, 0].astype(BF16), preferred_element_type=F32)

    def step_plain():
        compute()

    def step_scatter():
        start_window(x_copy, idxn_ref, (f - 1) * rx, rx)
        start_window(s_copy, idxp_ref, (f - 1) * rs, rs)
        compute()

    def step_gather():
        start_window(x_copy, idxn_ref, (f - 1) * rx, rx)
        start_window(a_copy, idxc_ref, (f - n_sc - 1) * rs, rs)
        compute()

    phase = jnp.where((f >= 1) & (f <= n_sc), 1, jnp.where((f > n_sc) & (f <= 2 * n_sc), 2, 0))
    lax.switch(phase, [step_plain, step_scatter, step_gather])

    @pl.when(f == n_f - 1)
    def _():
        wait_a()
        for bi in range(n_rows // cap):
            rows = slice(bi * cap, (bi + 1) * cap)
            ab_ref[rows, :] += (gate_ref[rows, :] * g2_ref[bi:bi + 1, :]) * y_ref[rows, :]

        @pl.when(e == n_e - 1)
        def _():
            wait_x()
            start_all(s_copy, idxc_ref)
            wait_s()


def _moe(idx_rows, h2ext, acc0, w_gate, w_up, w_down, g2, cap):
    d = acc0.shape[1]
    n_rows = idx_rows.shape[2]
    ff = w_gate.shape[3]
    tf = MOE_TF
    n_f = ff // tf
    n_x, n_sc = MOE_XGATHER_STEPS, MOE_SCATTER_STEPS
    assert n_x == 2 * n_sc and n_x + 1 <= n_f - 1 and n_rows % (n_x * SUBLANES) == 0
    est = (2 * 3 * d * tf * 4 + n_rows * h2ext.shape[1] * 4 + n_rows * d * 2 + n_rows * LANES * 4
           + 2 * n_rows * d * 4 + 3 * d * tf * 2 + 4 * n_rows * tf * 4 + (6 << 20))
    last = N_EXPERTS - 1
    return pl.pallas_call(
        functools.partial(_moe_kernel, n_f=n_f, n_rows=n_rows, d=d, cap=cap, n_x=n_x, n_sc=n_sc),
        grid=(N_EXPERTS, n_f),
        in_specs=[
            pl.BlockSpec((1, 1, n_rows), lambda e, f: (jnp.maximum(e - 1, 0), 0, 0), memory_space=pltpu.SMEM),
            pl.BlockSpec((1, 1, n_rows), lambda e, f: (e, 0, 0), memory_space=pltpu.SMEM),
            pl.BlockSpec((1, 1, n_rows), lambda e, f: (jnp.minimum(e + 1, last), 0, 0), memory_space=pltpu.SMEM),
            pl.BlockSpec(memory_space=pl.ANY),
            pl.BlockSpec(memory_space=pl.ANY),
            pl.BlockSpec((1, 1, d, tf), lambda e, f: (0, e, 0, f)),
            pl.BlockSpec((1, 1, d, tf), lambda e, f: (0, e, 0, f)),
            pl.BlockSpec((1, 1, tf, d), lambda e, f: (0, e, f, 0)),
            pl.BlockSpec((MOD_ROWS, d), lambda e, f: (0, 0)),
        ],
        out_specs=pl.BlockSpec(memory_space=pl.ANY),
        out_shape=jax.ShapeDtypeStruct(acc0.shape, F32),
        scratch_shapes=[
            pltpu.VMEM((n_rows, h2ext.shape[1]), F32),
            pltpu.VMEM((n_rows, d), BF16),
            pltpu.VMEM((n_rows, 1), F32),
            pltpu.VMEM((n_rows, d), F32),
            pltpu.VMEM((n_rows, d), F32),
            pltpu.SemaphoreType.DMA((3,)),
        ],
        input_output_aliases={4: 0},
        compiler_params=_cparams(("arbitrary", "arbitrary"), est),
        name="moe",
    )(idx_rows, idx_rows, idx_rows, h2ext, acc0, w_gate, w_up, w_down, g2)


def _final_kernel(a_ref, g_ref, b_ref, o_ref):
    o_ref[...] = _layer_norm(a_ref[...], g_ref[...], b_ref[...])


def _final(acc, ln_g, ln_b):
    n_rows, d = acc.shape
    tm = FINAL_TM
    return pl.pallas_call(
        _final_kernel,
        grid=(n_rows // tm,),
        in_specs=[
            pl.BlockSpec((tm, d), lambda i: (i, 0)),
            pl.BlockSpec((1, d), lambda i: (0, 0)),
            pl.BlockSpec((1, d), lambda i: (0, 0)),
        ],
        out_specs=pl.BlockSpec((tm, d), lambda i: (i, 0)),
        out_shape=jax.ShapeDtypeStruct((n_rows, d), F32),
        compiler_params=_cparams(("arbitrary",), 8 * tm * d * 4 + (8 << 20)),
        name="final_ln",
    )(acc, ln_g, ln_b)


def _rope_tables(n_tok):
    rows = n_tok // GRID_W
    row = np.broadcast_to(np.arange(rows, dtype=np.float64)[:, None], (rows, GRID_W)).reshape(-1)
    col = np.broadcast_to(np.arange(GRID_W, dtype=np.float64)[None, :], (rows, GRID_W)).reshape(-1)
    inv_freq = ROPE_THETA ** (-np.arange(0, ROPE_AXIS_DIM, 2, dtype=np.float64) / ROPE_AXIS_DIM)
    ang_r = row[:, None] * inv_freq[None, :]
    ang_c = col[:, None] * inv_freq[None, :]
    ang = np.concatenate([ang_r, ang_r, ang_c, ang_c], axis=-1)
    cos, sin = np.cos(ang).astype(np.float32), np.sin(ang).astype(np.float32)
    low = (np.arange(HEAD_DIM) % ROPE_AXIS_DIM) < ROPE_AXIS_DIM // 2
    sin_lo = np.where(low[None, :], -sin, np.float32(0.0))
    sin_hi = np.where(low[None, :], np.float32(0.0), sin)
    return jnp.asarray(cos), jnp.asarray(sin_lo), jnp.asarray(sin_hi)


def kernel(x, c, ctx, c_ctx, ada_w, ada_b, w_in, q_norm, k_norm, pool_w, pool_scale, w_out,
           ln1_g, ln1_b, router_w, w_gate, w_up, w_down, ln2_g, ln2_b):
    bsz, n_tok, d = x.shape
    n_ctx = ctx.shape[1]
    assert ada_w.shape[0] == DEPTH and d == D_MODEL and bsz + 1 <= MOD_ROWS
    cap = CAPACITY_FACTOR * n_tok // N_EXPERTS

    c8 = jnp.concatenate([c, c_ctx[None, :], jnp.zeros((MOD_ROWS - bsz - 1, d), F32)], axis=0)
    mod = _ada(c8, ada_w, ada_b)
    sh1, sc1, g1, sh2, sc2, g2 = [mod[:, k * d:(k + 1) * d] for k in range(6)]

    x2 = x.reshape(bsz * n_tok, d)
    cos, sin_lo, sin_hi = _rope_tables(n_tok)
    u, q, kt, v = _inproj(x2, sc1, sh1, w_in, q_norm, k_norm, cos, sin_lo, sin_hi, bsz, n_tok)
    kct, vc = _ctxkv(ctx.reshape(bsz * n_ctx, d), sc1, sh1, w_in, k_norm, bsz, n_ctx)

    attn = _attn(q, kt, v, kct, vc, bsz, n_tok, n_ctx)

    rw_pad = jnp.pad(router_w[0], ((0, 0), (0, LANES - N_EXPERTS)))
    acc0, h2ext = _outproj(u, pool_w, pool_scale, attn, x2, w_out, g1, sc2, sh2, ln1_g, ln1_b, rw_pad,
                           bsz, n_tok)

    idx = _topk(h2ext, bsz, n_tok, cap)
    idx_rows = (idx + (jnp.arange(bsz, dtype=jnp.int32) * n_tok)[:, None, None])
    idx_rows = idx_rows.transpose(1, 0, 2).reshape(N_EXPERTS, 1, bsz * cap)

    acc = _moe(idx_rows, h2ext, acc0, w_gate, w_up, w_down, g2, cap)
    out = _final(acc, ln2_g, ln2_b)
    return out.reshape(bsz, n_tok, d)
```

```python
import functools

import jax
import jax.numpy as jnp
import numpy as np
from jax import lax
from jax.experimental import pallas as pl
from jax.experimental.pallas import tpu as pltpu

F32 = jnp.float32
BF16 = jnp.bfloat16

D_MODEL = 2048
GRID_W = 64
HEAD_DIM = 128
ATTN_WIDTH = D_MODEL // 2
N_HEADS = ATTN_WIDTH // HEAD_DIM
N_KV_HEADS = 2
Q_PER_KV = N_HEADS // N_KV_HEADS
KV_WIDTH = N_KV_HEADS * HEAD_DIM
POOL_WIDTH = D_MODEL - ATTN_WIDTH
POOL_WINDOWS = (2, 4, 8, 16)
POOL_GROUP = POOL_WIDTH // len(POOL_WINDOWS)
Q_OFF = POOL_WIDTH
K_OFF = Q_OFF + ATTN_WIDTH
V_OFF = K_OFF + KV_WIDTH
IN_WIDTH = V_OFF + KV_WIDTH
ROPE_AXIS_DIM = HEAD_DIM // 2
ROPE_THETA = 10000.0
ATTN_SCALE = HEAD_DIM ** -0.5
Q_SCALE = ATTN_SCALE * float(np.log2(np.e))
V_EXT = 2 * HEAD_DIM
N_EXPERTS = 16
CAPACITY_FACTOR = 2
LN_EPS = 1e-6
QK_EPS = 1e-6
DEPTH = 1
ALPHA = (2.0 * DEPTH) ** 0.25

LANES = 128
SUBLANES = 8
V7X_VMEM_BYTES = 64 * 1024 * 1024
DMA_QUEUES = 2
MOD_ROWS = SUBLANES
HALO = SUBLANES

ADA_TN = 1024
SUB_ROWS = 256
INPROJ_TM = 512
ATTN_TQ = 512
ATTN_SUBTILES = 2
ATTN_TK = 512
OUTPROJ_TM = 512
MOE_TF = 512
MOE_XGATHER_STEPS = 8
MOE_SCATTER_STEPS = 4
FINAL_TM = 512
PREFIX_BLK = 256
COUNT_CHAINS = 8


def _vmem_limit(nbytes):
    return int(min(nbytes, V7X_VMEM_BYTES - 4 * 1024 * 1024))


def _cparams(sem, nbytes):
    return pltpu.CompilerParams(dimension_semantics=sem, vmem_limit_bytes=_vmem_limit(nbytes))


def _sigmoid(x):
    return 1.0 / (1.0 + jnp.exp(-x))


def _ada_kernel(c_ref, w_ref, b_ref, o_ref):
    cv = c_ref[...]
    s = cv * _sigmoid(cv)
    o_ref[...] = jnp.dot(s, w_ref[0], preferred_element_type=F32) + b_ref[...]


def _ada(c8, ada_w, ada_b):
    d = c8.shape[1]
    n_out = ada_w.shape[2]
    return pl.pallas_call(
        _ada_kernel,
        grid=(n_out // ADA_TN,),
        in_specs=[
            pl.BlockSpec((MOD_ROWS, d), lambda j: (0, 0)),
            pl.BlockSpec((1, d, ADA_TN), lambda j: (0, 0, j)),
            pl.BlockSpec((1, ADA_TN), lambda j: (0, j)),
        ],
        out_specs=pl.BlockSpec((MOD_ROWS, ADA_TN), lambda j: (0, j)),
        out_shape=jax.ShapeDtypeStruct((MOD_ROWS, n_out), F32),
        compiler_params=_cparams(("arbitrary",), 2 * d * ADA_TN * 4 + (8 << 20)),
        name="ada",
    )(c8, ada_w, ada_b)


def _ones_column(n):
    lane = lax.broadcasted_iota(jnp.int32, (n, V_EXT - HEAD_DIM), 1)
    return jnp.where(lane == 0, 1.0, 0.0)


def _rms(xh, g):
    ms = jnp.mean(xh * xh, axis=-1, keepdims=True)
    return xh * lax.rsqrt(ms + QK_EPS) * g


def _rope(xn, cos, sin_lo, sin_hi):
    return (xn * cos + pltpu.roll(xn, HEAD_DIM - ROPE_AXIS_DIM // 2, 1) * sin_lo
            + pltpu.roll(xn, ROPE_AXIS_DIM // 2, 1) * sin_hi)


def _inproj_kernel(x_ref, sc_ref, sh_ref, w_ref, qn_ref, kn_ref, cos_ref, slo_ref, shi_ref,
                   u_ref, q_ref, kt_ref, v_ref, *, tiles_per_batch, sub):
    b = pl.program_id(0) // tiles_per_batch
    sc = 1.0 + sc_ref[pl.ds(b, 1), :]
    sh = sh_ref[pl.ds(b, 1), :]
    qn = qn_ref[...]
    kn = kn_ref[...]
    for r0 in range(0, x_ref.shape[0], sub):
        rows = slice(r0, r0 + sub)
        h = x_ref[rows, :] * sc + sh
        proj = jnp.dot(h, w_ref[0], preferred_element_type=F32)
        u_ref[rows, :] = proj[:, :Q_OFF]
        cos = cos_ref[rows, :]
        slo = slo_ref[rows, :]
        shi = shi_ref[rows, :]
        for hd in range(N_HEADS):
            xh = proj[:, Q_OFF + hd * HEAD_DIM:Q_OFF + (hd + 1) * HEAD_DIM]
            qr = _rope(_rms(xh, qn), cos, slo, shi) * Q_SCALE
            q_ref[0, hd, rows, :] = qr
        for hd in range(N_KV_HEADS):
            kh = proj[:, K_OFF + hd * HEAD_DIM:K_OFF + (hd + 1) * HEAD_DIM]
            kr = _rope(_rms(kh, kn), cos, slo, shi)
            kt_ref[0, hd, :, rows] = kr.T
            v_ref[0, hd, rows, :HEAD_DIM] = proj[:, V_OFF + hd * HEAD_DIM:V_OFF + (hd + 1) * HEAD_DIM]
            v_ref[0, hd, rows, HEAD_DIM:] = _ones_column(sub)


def _inproj(x2, sc1, sh1, w_in, q_norm, k_norm, cos, sin_lo, sin_hi, bsz, n_tok):
    tm = INPROJ_TM
    tpb = n_tok // tm
    d = x2.shape[1]
    est = (2 * tm * d * 4 + d * IN_WIDTH * 4 + 2 * tm * IN_WIDTH * 4 + 2 * tm * IN_WIDTH * 4
           + (8 << 20))
    return pl.pallas_call(
        functools.partial(_inproj_kernel, tiles_per_batch=tpb, sub=SUB_ROWS),
        grid=(bsz * tpb,),
        in_specs=[
            pl.BlockSpec((tm, d), lambda i: (i, 0)),
            pl.BlockSpec((MOD_ROWS, d), lambda i: (0, 0)),
            pl.BlockSpec((MOD_ROWS, d), lambda i: (0, 0)),
            pl.BlockSpec((1, d, IN_WIDTH), lambda i: (0, 0, 0)),
            pl.BlockSpec((1, HEAD_DIM), lambda i: (0, 0)),
            pl.BlockSpec((1, HEAD_DIM), lambda i: (0, 0)),
            pl.BlockSpec((tm, HEAD_DIM), lambda i: (i % tpb, 0)),
            pl.BlockSpec((tm, HEAD_DIM), lambda i: (i % tpb, 0)),
            pl.BlockSpec((tm, HEAD_DIM), lambda i: (i % tpb, 0)),
        ],
        out_specs=[
            pl.BlockSpec((tm, POOL_WIDTH), lambda i: (i, 0)),
            pl.BlockSpec((1, N_HEADS, tm, HEAD_DIM), lambda i: (i // tpb, 0, i % tpb, 0)),
            pl.BlockSpec((1, N_KV_HEADS, HEAD_DIM, tm), lambda i: (i // tpb, 0, 0, i % tpb)),
            pl.BlockSpec((1, N_KV_HEADS, tm, V_EXT), lambda i: (i // tpb, 0, i % tpb, 0)),
        ],
        out_shape=[
            jax.ShapeDtypeStruct((bsz * n_tok, POOL_WIDTH), F32),
            jax.ShapeDtypeStruct((bsz, N_HEADS, n_tok, HEAD_DIM), F32),
            jax.ShapeDtypeStruct((bsz, N_KV_HEADS, HEAD_DIM, n_tok), F32),
            jax.ShapeDtypeStruct((bsz, N_KV_HEADS, n_tok, V_EXT), F32),
        ],
        compiler_params=_cparams(("arbitrary",), est),
        name="inproj",
    )(x2, sc1, sh1, w_in, q_norm, k_norm, cos, sin_lo, sin_hi)


def _ctxkv_kernel(ctx_ref, sc_ref, sh_ref, w_ref, kn_ref, kt_ref, v_ref, *, bsz, n_ctx):
    hc = ctx_ref[...] * (1.0 + sc_ref[2:3, :]) + sh_ref[2:3, :]
    kv = jnp.dot(hc, w_ref[0], preferred_element_type=F32)
    kn = kn_ref[...]
    for b in range(bsz):
        rows = slice(b * n_ctx, (b + 1) * n_ctx)
        for hd in range(N_KV_HEADS):
            kh = kv[rows, hd * HEAD_DIM:(hd + 1) * HEAD_DIM]
            kt_ref[b, hd] = _rms(kh, kn).T
            v_ref[b, hd, :, :HEAD_DIM] = kv[rows, KV_WIDTH + hd * HEAD_DIM:KV_WIDTH + (hd + 1) * HEAD_DIM]
            v_ref[b, hd, :, HEAD_DIM:] = _ones_column(n_ctx)


def _ctxkv(ctx2, sc1, sh1, w_in, k_norm, bsz, n_ctx):
    d = ctx2.shape[1]
    kvw = 2 * KV_WIDTH
    return pl.pallas_call(
        functools.partial(_ctxkv_kernel, bsz=bsz, n_ctx=n_ctx),
        grid=(1,),
        in_specs=[
            pl.BlockSpec((bsz * n_ctx, d), lambda i: (0, 0)),
            pl.BlockSpec((MOD_ROWS, d), lambda i: (0, 0)),
            pl.BlockSpec((MOD_ROWS, d), lambda i: (0, 0)),
            pl.BlockSpec((1, d, kvw), lambda i: (0, 0, K_OFF // kvw)),
            pl.BlockSpec((1, HEAD_DIM), lambda i: (0, 0)),
        ],
        out_specs=[
            pl.BlockSpec((bsz, N_KV_HEADS, HEAD_DIM, n_ctx), lambda i: (0, 0, 0, 0)),
            pl.BlockSpec((bsz, N_KV_HEADS, n_ctx, V_EXT), lambda i: (0, 0, 0, 0)),
        ],
        out_shape=[
            jax.ShapeDtypeStruct((bsz, N_KV_HEADS, HEAD_DIM, n_ctx), F32),
            jax.ShapeDtypeStruct((bsz, N_KV_HEADS, n_ctx, V_EXT), F32),
        ],
        compiler_params=_cparams(("arbitrary",), 2 * (bsz * n_ctx * d * 4 + d * kvw * 4) + (8 << 20)),
        name="ctxkv",
    )(ctx2, sc1, sh1, w_in, k_norm)


def _attn_kernel(q_ref, kt_ref, v_ref, kct_ref, vc_ref, o_ref, *, tq, tk, n_tok, n_sub):
    ts = tq // n_sub
    rows = Q_PER_KV * ts
    qs = [q_ref[0, :, i * ts:(i + 1) * ts, :].reshape(rows, HEAD_DIM) for i in range(n_sub)]

    def step(q, carry, kt_c, v_c):
        m, acc = carry
        s = jnp.dot(q, kt_c, preferred_element_type=F32)
        m_new = jnp.maximum(m, jnp.max(s, axis=-1, keepdims=True))
        alpha = jnp.exp2(m - m_new)
        p = jnp.exp2(s - m_new)
        acc = alpha * acc + jnp.dot(p, v_c, preferred_element_type=F32)
        return m_new, acc

    carries = [(jnp.full((rows, 1), -jnp.inf, F32), jnp.zeros((rows, V_EXT), F32)) for _ in range(n_sub)]
    carries = [step(q, c, kct_ref[0, 0], vc_ref[0, 0]) for q, c in zip(qs, carries)]
    for j in range(n_tok // tk):
        carries = [step(q, c, kt_ref[0, 0, :, j * tk:(j + 1) * tk], v_ref[0, 0, j * tk:(j + 1) * tk, :])
                   for q, c in zip(qs, carries)]
    for i, (_, acc) in enumerate(carries):
        out = acc[:, :HEAD_DIM] / acc[:, HEAD_DIM:HEAD_DIM + 1]
        for g in range(Q_PER_KV):
            o_ref[i * ts:(i + 1) * ts, g * HEAD_DIM:(g + 1) * HEAD_DIM] = out[g * ts:(g + 1) * ts]


def _attn(q, kt, v, kct, vc, bsz, n_tok, n_ctx):
    tq, tk = ATTN_TQ, ATTN_TK
    nq = n_tok // tq
    gw = Q_PER_KV * HEAD_DIM
    return pl.pallas_call(
        functools.partial(_attn_kernel, tq=tq, tk=tk, n_tok=n_tok, n_sub=ATTN_SUBTILES),
        grid=(bsz, N_KV_HEADS, nq),
        in_specs=[
            pl.BlockSpec((1, Q_PER_KV, tq, HEAD_DIM), lambda b, h, i: (b, h, i, 0)),
            pl.BlockSpec((1, 1, HEAD_DIM, n_tok), lambda b, h, i: (b, h, 0, 0)),
            pl.BlockSpec((1, 1, n_tok, V_EXT), lambda b, h, i: (b, h, 0, 0)),
            pl.BlockSpec((1, 1, HEAD_DIM, n_ctx), lambda b, h, i: (b, h, 0, 0)),
            pl.BlockSpec((1, 1, n_ctx, V_EXT), lambda b, h, i: (b, h, 0, 0)),
        ],
        out_specs=pl.BlockSpec((tq, gw), lambda b, h, i: (b * nq + i, h)),
        out_shape=jax.ShapeDtypeStruct((bsz * n_tok, ATTN_WIDTH), F32),
        compiler_params=_cparams(("arbitrary", "arbitrary", "arbitrary"), 40 << 20),
        name="attn",
    )(q, kt, v, kct, vc)


def _layer_norm(r, g, b):
    mu = jnp.mean(r, axis=-1, keepdims=True)
    rc = r - mu
    var = jnp.mean(rc * rc, axis=-1, keepdims=True)
    return rc * lax.rsqrt(var + LN_EPS) * g + b


def _window_sums(ext_ref, q_ref, r0, sub, cols, w):
    if w == 2:
        return ext_ref[HALO + r0 - 1:HALO + r0 - 1 + sub, cols] + ext_ref[HALO + r0:HALO + r0 + sub, cols]
    l2, l4, l8 = sub + 3 * HALO, sub + 2 * HALO, sub + HALO
    q_ref[0, 0:l2, :] = ext_ref[r0:r0 + l2, cols] + ext_ref[r0 + 1:r0 + 1 + l2, cols]
    q_ref[1, 0:l4, :] = q_ref[0, 0:l4, :] + q_ref[0, 2:2 + l4, :]
    if w == 4:
        return q_ref[1, HALO - 2:HALO - 2 + sub, :]
    q_ref[2, 0:l8, :] = q_ref[1, 0:l8, :] + q_ref[1, 4:4 + l8, :]
    if w == 8:
        return q_ref[2, HALO - 4:HALO - 4 + sub, :]
    assert w == 2 * HALO
    return q_ref[2, 0:sub, :] + q_ref[2, HALO:HALO + sub, :]


def _pool_mix(ext_ref, q_ref, uc_ref, pw_ref, ps_ref, t0, r0, sub, n_tok, gi):
    half = POOL_WINDOWS[gi] // 2
    cols = slice(gi * POOL_GROUP, (gi + 1) * POOL_GROUP)
    acc = _window_sums(ext_ref, q_ref, r0, sub, cols, POOL_WINDOWS[gi])
    t = t0 + r0 + lax.broadcasted_iota(jnp.int32, (sub, POOL_GROUP), 0)
    cnt = (jnp.minimum(t + half, n_tok) - jnp.maximum(t - half, 0)).astype(F32)
    diff = acc / cnt - uc_ref[r0:r0 + sub, cols]
    return jnp.dot(diff, pw_ref[0, gi], preferred_element_type=F32) * ps_ref[:, cols]


def _outproj_kernel(up_ref, uc_ref, un_ref, pw_ref, ps_ref, attn_ref, x_ref, w_ref, g1_ref, sc_ref, sh_ref,
                    lg_ref, lb_ref, rw_ref, acc_ref, h_ref, ext_ref, q_ref, *, tiles_per_batch, sub, n_tok):
    b = pl.program_id(0) // tiles_per_batch
    it = pl.program_id(0) % tiles_per_batch
    tm, d = x_ref.shape
    ext_ref[0:HALO, :] = jnp.where(it == 0, 0.0, up_ref[...])
    ext_ref[HALO:HALO + tm, :] = uc_ref[...]
    ext_ref[HALO + tm:HALO + tm + HALO, :] = jnp.where(it == tiles_per_batch - 1, 0.0, un_ref[...])
    ext_ref[tm + 2 * HALO:, :] = jnp.zeros((ext_ref.shape[0] - tm - 2 * HALO, POOL_WIDTH), F32)
    g1 = g1_ref[pl.ds(b, 1), :]
    sc = 1.0 + sc_ref[pl.ds(b, 1), :]
    sh = sh_ref[pl.ds(b, 1), :]
    rw = rw_ref[...]
    for r0 in range(0, tm, sub):
        rows = slice(r0, r0 + sub)
        a = attn_ref[rows, :]
        y = jnp.concatenate([jnp.dot(a, w_ref[0, POOL_WIDTH:, k * (d // 2):(k + 1) * (d // 2)],
                                     preferred_element_type=F32) for k in range(2)], axis=1)
        for gi in range(len(POOL_WINDOWS)):
            mixed = _pool_mix(ext_ref, q_ref, uc_ref, pw_ref, ps_ref, it * tm, r0, sub, n_tok, gi)
            y = y + jnp.dot(mixed, w_ref[0, gi * POOL_GROUP:(gi + 1) * POOL_GROUP, :], preferred_element_type=F32)
        r = ALPHA * x_ref[rows, :] + g1 * y
        xm = _layer_norm(r, lg_ref[...], lb_ref[...])
        acc_ref[rows, :] = ALPHA * xm
        h2 = xm * sc + sh
        h_ref[rows, :d] = h2
        logits = jnp.dot(h2, rw, preferred_element_type=F32)
        valid = lax.broadcasted_iota(jnp.int32, logits.shape, 1) < N_EXPERTS
        lg = jnp.where(valid, logits, -jnp.inf)
        ex = jnp.exp(lg - jnp.max(lg, axis=-1, keepdims=True))
        h_ref[rows, d:] = ex / jnp.sum(ex, axis=-1, keepdims=True)


def _outproj(u, pool_w, pool_scale, attn, x2, w_out, g1, sc2, sh2, ln_g, ln_b, rw_pad, bsz, n_tok):
    tm = OUTPROJ_TM
    tpb = n_tok // tm
    d = x2.shape[1]
    n_rows = bsz * n_tok
    hb = tm // HALO
    last_hblk = n_rows // HALO - 1
    est = d * d * 4 + 2 * tm * d * 4 * 4 + 8 * tm * d * 4 + (8 << 20)
    return pl.pallas_call(
        functools.partial(_outproj_kernel, tiles_per_batch=tpb, sub=SUB_ROWS, n_tok=n_tok),
        grid=(bsz * tpb,),
        in_specs=[
            pl.BlockSpec((HALO, POOL_WIDTH), lambda i: (jnp.maximum(i * hb - 1, 0), 0)),
            pl.BlockSpec((tm, POOL_WIDTH), lambda i: (i, 0)),
            pl.BlockSpec((HALO, POOL_WIDTH), lambda i: (jnp.minimum((i + 1) * hb, last_hblk), 0)),
            pl.BlockSpec((1, len(POOL_WINDOWS), POOL_GROUP, POOL_GROUP), lambda i: (0, 0, 0, 0)),
            pl.BlockSpec((1, POOL_WIDTH), lambda i: (0, 0)),
            pl.BlockSpec((tm, ATTN_WIDTH), lambda i: (i, 0)),
            pl.BlockSpec((tm, d), lambda i: (i, 0)),
            pl.BlockSpec((1, d, d), lambda i: (0, 0, 0)),
            pl.BlockSpec((MOD_ROWS, d), lambda i: (0, 0)),
            pl.BlockSpec((MOD_ROWS, d), lambda i: (0, 0)),
            pl.BlockSpec((MOD_ROWS, d), lambda i: (0, 0)),
            pl.BlockSpec((1, d), lambda i: (0, 0)),
            pl.BlockSpec((1, d), lambda i: (0, 0)),
            pl.BlockSpec((d, LANES), lambda i: (0, 0)),
        ],
        out_specs=[
            pl.BlockSpec((tm, d), lambda i: (i, 0)),
            pl.BlockSpec((tm, d + LANES), lambda i: (i, 0)),
        ],
        out_shape=[
            jax.ShapeDtypeStruct((n_rows, d), F32),
            jax.ShapeDtypeStruct((n_rows, d + LANES), F32),
        ],
        scratch_shapes=[pltpu.VMEM((tm + 4 * HALO, POOL_WIDTH), F32),
                        pltpu.VMEM((3, SUB_ROWS + 3 * HALO, POOL_GROUP), F32)],
        compiler_params=_cparams(("arbitrary",), est),
        name="outproj",
    )(u, u, u, pool_w, pool_scale, attn, x2, w_out, g1, sc2, sh2, ln_g, ln_b, rw_pad)


def _prefix_incl(flags_bf16, out_ref, n_tok):
    blk = PREFIX_BLK
    tri = (lax.broadcasted_iota(jnp.int32, (blk, blk), 0)
           >= lax.broadcasted_iota(jnp.int32, (blk, blk), 1)).astype(BF16)
    carry = jnp.zeros((1, LANES), F32)
    for k in range(n_tok // blk):
        p = jnp.dot(tri, flags_bf16[k * blk:(k + 1) * blk], preferred_element_type=F32) + carry
        out_ref[k * blk:(k + 1) * blk, :] = p
        carry = p[blk - 1:blk, :]


def _topk_kernel(aff_ref, idx_ref, cnt_ref, part_ref, *, n_tok, cap):
    lane = lax.broadcasted_iota(jnp.int32, (1, LANES), 1)
    bits = jnp.where(lane < N_EXPERTS, pltpu.bitcast(aff_ref[...], jnp.int32), -1)

    seg = n_tok // COUNT_CHAINS

    def search(_, carry):
        lo, hi = carry
        mid = lo + ((hi - lo + 1) >> 1)
        n_ge = sum(jnp.sum((bits[k * seg:(k + 1) * seg] >= mid).astype(jnp.int32), axis=0, keepdims=True)
                   for k in range(COUNT_CHAINS))
        ok = n_ge >= cap
        return jnp.where(ok, mid, lo), jnp.where(ok, hi, mid - 1)

    lo0 = jnp.zeros((1, LANES), jnp.int32)
    hi0 = jnp.full((1, LANES), 0x7F800000, jnp.int32)
    thr, _ = lax.fori_loop(0, 32, search, (lo0, hi0))

    gt = bits > thr
    eq = bits == thr
    need = (cap - jnp.sum(gt.astype(jnp.int32), axis=0, keepdims=True)).astype(F32)
    _prefix_incl(jnp.where(eq, 1.0, 0.0).astype(BF16), cnt_ref, n_tok)
    sel = gt | (eq & (cnt_ref[...] <= need))
    _prefix_incl(jnp.where(sel, 1.0, 0.0).astype(BF16), cnt_ref, n_tok)

    slot = lax.broadcasted_iota(jnp.int32, (1, cap), 1).astype(F32)
    blk = PREFIX_BLK
    part_ref[...] = jnp.zeros_like(part_ref)

    def count(k, _):
        r0 = pl.multiple_of(k * blk, blk)
        cblk = cnt_ref[pl.ds(r0, blk), :]
        for e in range(N_EXPERTS):
            hit = (cblk[:, e:e + 1] <= slot).astype(F32)
            part_ref[e] += jnp.sum(hit.reshape(blk // SUBLANES, SUBLANES, cap), axis=0)
        return 0

    lax.fori_loop(0, n_tok // blk, count, 0)
    for e in range(N_EXPERTS):
        idx_ref[0, e:e + 1, :] = jnp.sum(part_ref[e], axis=0, keepdims=True).astype(jnp.int32)


def _topk(h2ext, bsz, n_tok, cap):
    aff_blk = h2ext.shape[1] // LANES - 1
    return pl.pallas_call(
        functools.partial(_topk_kernel, n_tok=n_tok, cap=cap),
        grid=(bsz,),
        in_specs=[pl.BlockSpec((n_tok, LANES), lambda b: (b, aff_blk))],
        out_specs=pl.BlockSpec((1, N_EXPERTS, cap), lambda b: (b, 0, 0)),
        out_shape=jax.ShapeDtypeStruct((bsz, N_EXPERTS, cap), jnp.int32),
        scratch_shapes=[pltpu.VMEM((n_tok, LANES), F32), pltpu.VMEM((N_EXPERTS, SUBLANES, cap), F32)],
        compiler_params=_cparams(("arbitrary",), 32 << 20),
        name="topk",
    )(h2ext)


def _moe_kernel(idxp_ref, idxc_ref, idxn_ref, h_hbm, acc_in_hbm, wg_ref, wu_ref, wd_ref, g2_ref, acc_hbm,
                xe_ref, xb_ref, gate_ref, y_ref, ab_ref, sem, *, n_f, n_rows, d, cap, n_x, n_sc):
    del acc_in_hbm
    e = pl.program_id(0)
    f = pl.program_id(1)
    n_e = pl.num_programs(0)
    sem_x, sem_a, sem_s = sem.at[0], sem.at[1], sem.at[2]
    rx = n_rows // n_x
    rs = n_rows // n_sc

    def x_copy(idx_ref, g0, n, k):
        return pltpu.make_async_copy(h_hbm.at[pl.ds(idx_ref[0, 0, g0 + k], 1), :],
                                     xe_ref.at[pl.ds(g0, n), :].at[pl.ds(k, 1), :], sem_x)

    def a_copy(idx_ref, g0, n, k):
        return pltpu.make_async_copy(acc_hbm.at[pl.ds(idx_ref[0, 0, g0 + k], 1), :],
                                     ab_ref.at[pl.ds(g0, n), :].at[pl.ds(k, 1), :], sem_a)

    def s_copy(idx_ref, g0, n, k):
        return pltpu.make_async_copy(ab_ref.at[pl.ds(g0, n), :].at[pl.ds(k, 1), :],
                                     acc_hbm.at[pl.ds(idx_ref[0, 0, g0 + k], 1), :], sem_s)

    def start_window(make, idx_ref, j0, n, n_queues=1):
        g0 = pl.multiple_of(j0, SUBLANES)
        for k in range(n):
            make(idx_ref, g0, n, k).start(priority=k % n_queues)

    def start_all(make, idx_ref):
        def body(g, _):
            start_window(make, idx_ref, g * SUBLANES, SUBLANES, n_queues=DMA_QUEUES)
            return 0

        lax.fori_loop(0, n_rows // SUBLANES, body, 0)

    def wait_x():
        pltpu.make_async_copy(h_hbm.at[pl.ds(0, n_rows), :], xe_ref, sem_x).wait()

    def wait_a():
        pltpu.make_async_copy(acc_hbm.at[pl.ds(0, n_rows), :], ab_ref, sem_a).wait()

    def wait_s():
        pltpu.make_async_copy(ab_ref, acc_hbm.at[pl.ds(0, n_rows), :], sem_s).wait()

    @pl.when((e == 0) & (f == 0))
    def _():
        start_all(x_copy, idxc_ref)
        start_all(a_copy, idxc_ref)
        wait_a()

    @pl.when(f == 0)
    def _():
        wait_x()
        xb_ref[...] = xe_ref[:, :d].astype(BF16)
        lane = lax.broadcasted_iota(jnp.int32, (n_rows, LANES), 1)
        gate_ref[...] = jnp.sum(jnp.where(lane == e, xe_ref[:, d:], 0.0), axis=-1, keepdims=True)
        y_ref[...] = jnp.zeros_like(y_ref)

    @pl.when(f == n_sc + 1)
    def _():
        wait_s()

    def compute():
        x = xb_ref[...]
        a = jnp.dot(x, wg_ref[0, 0].astype(BF16), preferred_element_type=F32)
        b = jnp.dot(x, wu_ref[0, 0].astype(BF16), preferred_element_type=F32)
        hm = (a * _sigmoid(a) * b).astype(BF16)
        y_ref[...] += jnp.dot(hm, wd_ref[0, 0].astype(BF16), preferred_element_type=F32)

    def step_plain():
        compute()

    def step_scatter():
        start_window(x_copy, idxn_ref, (f - 1) * rx, rx)
        start_window(s_copy, idxp_ref, (f - 1) * rs, rs)
        compute()

    def step_gather():
        start_window(x_copy, idxn_ref, (f - 1) * rx, rx)
        start_window(a_copy, idxc_ref, (f - n_sc - 1) * rs, rs)
        compute()

    phase = jnp.where((f >= 1) & (f <= n_sc), 1, jnp.where((f > n_sc) & (f <= 2 * n_sc), 2, 0))
    lax.switch(phase, [step_plain, step_scatter, step_gather])

    @pl.when(f == n_f - 1)
    def _():
        wait_a()
        for bi in range(n_rows // cap):
            rows = slice(bi * cap, (bi + 1) * cap)
            ab_ref[rows, :] += (gate_ref[rows, :] * g2_ref[bi:bi + 1, :]) * y_ref[rows, :]

        @pl.when(e == n_e - 1)
        def _():
            wait_x()
            start_all(s_copy, idxc_ref)
            wait_s()


def _moe(idx_rows, h2ext, acc0, w_gate, w_up, w_down, g2, cap):
    d = acc0.shape[1]
    n_rows = idx_rows.shape[2]
    ff = w_gate.shape[3]
    tf = MOE_TF
    n_f = ff // tf
    n_x, n_sc = MOE_XGATHER_STEPS, MOE_SCATTER_STEPS
    assert n_x == 2 * n_sc and n_x + 1 <= n_f - 1 and n_rows % (n_x * SUBLANES) == 0
    est = (2 * 3 * d * tf * 4 + n_rows * h2ext.shape[1] * 4 + n_rows * d * 2 + n_rows * LANES * 4
           + 2 * n_rows * d * 4 + 3 * d * tf * 2 + 4 * n_rows * tf * 4 + (6 << 20))
    last = N_EXPERTS - 1
    return pl.pallas_call(
        functools.partial(_moe_kernel, n_f=n_f, n_rows=n_rows, d=d, cap=cap, n_x=n_x, n_sc=n_sc),
        grid=(N_EXPERTS, n_f),
        in_specs=[
            pl.BlockSpec((1, 1, n_rows), lambda e, f: (jnp.maximum(e - 1, 0), 0, 0), memory_space=pltpu.SMEM),
            pl.BlockSpec((1, 1, n_rows), lambda e, f: (e, 0, 0), memory_space=pltpu.SMEM),
            pl.BlockSpec((1, 1, n_rows), lambda e, f: (jnp.minimum(e + 1, last), 0, 0), memory_space=pltpu.SMEM),
            pl.BlockSpec(memory_space=pl.ANY),
            pl.BlockSpec(memory_space=pl.ANY),
            pl.BlockSpec((1, 1, d, tf), lambda e, f: (0, e, 0, f)),
            pl.BlockSpec((1, 1, d, tf), lambda e, f: (0, e, 0, f)),
            pl.BlockSpec((1, 1, tf, d), lambda e, f: (0, e, f, 0)),
            pl.BlockSpec((MOD_ROWS, d), lambda e, f: (0, 0)),
        ],
        out_specs=pl.BlockSpec(memory_space=pl.ANY),
        out_shape=jax.ShapeDtypeStruct(acc0.shape, F32),
        scratch_shapes=[
            pltpu.VMEM((n_rows, h2ext.shape[1]), F32),
            pltpu.VMEM((n_rows, d), BF16),
            pltpu.VMEM((n_rows, 1), F32),
            pltpu.VMEM((n_rows, d), F32),
            pltpu.VMEM((n_rows, d), F32),
            pltpu.SemaphoreType.DMA((3,)),
        ],
        input_output_aliases={4: 0},
        compiler_params=_cparams(("arbitrary", "arbitrary"), est),
        name="moe",
    )(idx_rows, idx_rows, idx_rows, h2ext, acc0, w_gate, w_up, w_down, g2)


def _final_kernel(a_ref, g_ref, b_ref, o_ref):
    o_ref[...] = _layer_norm(a_ref[...], g_ref[...], b_ref[...])


def _final(acc, ln_g, ln_b):
    n_rows, d = acc.shape
    tm = FINAL_TM
    return pl.pallas_call(
        _final_kernel,
        grid=(n_rows // tm,),
        in_specs=[
            pl.BlockSpec((tm, d), lambda i: (i, 0)),
            pl.BlockSpec((1, d), lambda i: (0, 0)),
            pl.BlockSpec((1, d), lambda i: (0, 0)),
        ],
        out_specs=pl.BlockSpec((tm, d), lambda i: (i, 0)),
        out_shape=jax.ShapeDtypeStruct((n_rows, d), F32),
        compiler_params=_cparams(("arbitrary",), 8 * tm * d * 4 + (8 << 20)),
        name="final_ln",
    )(acc, ln_g, ln_b)


def _rope_tables(n_tok):
    rows = n_tok // GRID_W
    row = np.broadcast_to(np.arange(rows, dtype=np.float64)[:, None], (rows, GRID_W)).reshape(-1)
    col = np.broadcast_to(np.arange(GRID_W, dtype=np.float64)[None, :], (rows, GRID_W)).reshape(-1)
    inv_freq = ROPE_THETA ** (-np.arange(0, ROPE_AXIS_DIM, 2, dtype=np.float64) / ROPE_AXIS_DIM)
    ang_r = row[:, None] * inv_freq[None, :]
    ang_c = col[:, None] * inv_freq[None, :]
    ang = np.concatenate([ang_r, ang_r, ang_c, ang_c], axis=-1)
    cos, sin = np.cos(ang).astype(np.float32), np.sin(ang).astype(np.float32)
    low = (np.arange(HEAD_DIM) % ROPE_AXIS_DIM) < ROPE_AXIS_DIM // 2
    sin_lo = np.where(low[None, :], -sin, np.float32(0.0))
    sin_hi = np.where(low[None, :], np.float32(0.0), sin)
    return jnp.asarray(cos), jnp.asarray(sin_lo), jnp.asarray(sin_hi)


def kernel(x, c, ctx, c_ctx, ada_w, ada_b, w_in, q_norm, k_norm, pool_w, pool_scale, w_out,
           ln1_g, ln1_b, router_w, w_gate, w_up, w_down, ln2_g, ln2_b):
    bsz, n_tok, d = x.shape
    n_ctx = ctx.shape[1]
    assert ada_w.shape[0] == DEPTH and d == D_MODEL and bsz + 1 <= MOD_ROWS
    cap = CAPACITY_FACTOR * n_tok // N_EXPERTS

    c8 = jnp.concatenate([c, c_ctx[None, :], jnp.zeros((MOD_ROWS - bsz - 1, d), F32)], axis=0)
    mod = _ada(c8, ada_w, ada_b)
    sh1, sc1, g1, sh2, sc2, g2 = [mod[:, k * d:(k + 1) * d] for k in range(6)]

    x2 = x.reshape(bsz * n_tok, d)
    cos, sin_lo, sin_hi = _rope_tables(n_tok)
    u, q, kt, v = _inproj(x2, sc1, sh1, w_in, q_norm, k_norm, cos, sin_lo, sin_hi, bsz, n_tok)
    kct, vc = _ctxkv(ctx.reshape(bsz * n_ctx, d), sc1, sh1, w_in, k_norm, bsz, n_ctx)

    attn = _attn(q, kt, v, kct, vc, bsz, n_tok, n_ctx)

    rw_pad = jnp.pad(router_w[0], ((0, 0), (0, LANES - N_EXPERTS)))
    acc0, h2ext = _outproj(u, pool_w, pool_scale, attn, x2, w_out, g1, sc2, sh2, ln1_g, ln1_b, rw_pad,
                           bsz, n_tok)

    idx = _topk(h2ext, bsz, n_tok, cap)
    idx_rows = (idx + (jnp.arange(bsz, dtype=jnp.int32) * n_tok)[:, None, None])
    idx_rows = idx_rows.transpose(1, 0, 2).reshape(N_EXPERTS, 1, bsz * cap)

    acc = _moe(idx_rows, h2ext, acc0, w_gate, w_up, w_down, g2, cap)
    out = _final(acc, ln2_g, ln2_b)
    return out.reshape(bsz, n_tok, d)
```
